```python
import math
import jax, jax.numpy as jnp
from jax import lax
import numpy as np

D_MODEL = 1024
BATCH = 8
SEQ = 8192
DEPTH = 2

N_A_LAYERS = DEPTH // 2
N_B_LAYERS = DEPTH - N_A_LAYERS
CHUNK = 128
A_WIDTH = D_MODEL
A_GROUPS = 8
A_GROUP_DIM = A_WIDTH // A_GROUPS
HEAD_DIM = 64
N_Q_HEADS = D_MODEL // HEAD_DIM
N_KV_HEADS = 4
GQA_GROUP = N_Q_HEADS // N_KV_HEADS
WINDOW = 128
BLOCK = 128
D_FF = 2816
CONV_WIDTH = 3
PLE_DIM = 256
EPS = 1e-6

kernel_name = "yoco_gmlp_swa_sink_hybrid"


def _alibi_slopes(n):
    return np.array([2.0 ** (-8.0 * (h + 1) / n) for h in range(n)], dtype=np.float32)


def rmsnorm(x, g):
    xf = x.astype(jnp.float32)
    y = xf * lax.rsqrt(jnp.mean(xf * xf, axis=-1, keepdims=True) + EPS)
    return (y * g.astype(jnp.float32)).astype(x.dtype)


def mixer_a(xn, w_in, g_v, w_s, b_s, w_out):
    B, S, _ = xn.shape
    z = jax.nn.gelu(xn @ w_in)
    u, v = jnp.split(z, 2, axis=-1)
    v = rmsnorm(v, g_v)
    nc = S // CHUNK
    v = v.reshape(B, nc, CHUNK, A_GROUPS, A_GROUP_DIM)
    causal = jnp.tril(jnp.ones((CHUNK, CHUNK), dtype=bool))
    w = jnp.where(causal[None], w_s, jnp.zeros((), w_s.dtype)).astype(v.dtype)
    s = jnp.einsum('hts,bcshd->bcthd', w, v) + b_s.T.astype(v.dtype)[None, None, :, :, None]
    s = s.reshape(B, S, A_WIDTH)
    return (u * s) @ w_out


def shared_kv(h, g, w_kv):
    B, S, _ = h.shape
    nb = S // BLOCK
    kv = rmsnorm(h, g) @ w_kv
    k, v = jnp.split(kv, 2, axis=-1)

    def band(t):
        tb = t.reshape(B, nb, BLOCK, N_KV_HEADS, HEAD_DIM)
        prev = jnp.pad(tb[:, :-1], ((0, 0), (1, 0), (0, 0), (0, 0), (0, 0)))
        return jnp.concatenate([prev, tb], axis=2)

    return band(k), band(v)


def mixer_b(xn, w_q, sinks, w_o, kblk, vblk):
    B, S, _ = xn.shape
    nb = S // BLOCK
    q = (xn @ w_q).reshape(B, nb, BLOCK, N_KV_HEADS, GQA_GROUP, HEAD_DIM)
    scores = jnp.einsum('bnikgd,bnjkd->bnkgij', q.astype(jnp.float32),
                        kblk.astype(jnp.float32)) * (HEAD_DIM ** -0.5)
    i = jnp.arange(BLOCK)[:, None]
    j = jnp.arange(2 * BLOCK)[None, :]
    dist = i + BLOCK - j
    in_band = (dist >= 0) & (dist < WINDOW)
    blk = jnp.arange(nb)[:, None, None]
    valid = in_band[None] & ((blk > 0) | (j[None] >= BLOCK))
    slopes = jnp.asarray(_alibi_slopes(N_Q_HEADS)).reshape(N_KV_HEADS, GQA_GROUP)
    scores = scores - slopes[:, :, None, None] * dist.astype(jnp.float32)
    scores = jnp.where(valid[None, :, None, None], scores, -jnp.inf)
    sink = sinks.astype(jnp.float32).reshape(N_KV_HEADS, GQA_GROUP)[:, :, None, None]
    m = jnp.maximum(jnp.max(scores, axis=-1, keepdims=True), sink)
    e = jnp.exp(scores - m)
    probs = e / (jnp.sum(e, axis=-1, keepdims=True) + jnp.exp(sink - m))
    out = jnp.einsum('bnkgij,bnjkd->bnikgd', probs.astype(vblk.dtype), vblk)
    return out.reshape(B, S, N_Q_HEADS * HEAD_DIM) @ w_o


def conv_ffn(xn, w_up, conv_w, conv_b, w_down):
    S = xn.shape[1]
    h = xn @ w_up
    hp = jnp.pad(h, ((0, 0), (CONV_WIDTH - 1, 0), (0, 0)))
    c = conv_b + sum(hp[:, t:t + S] * conv_w[t] for t in range(CONV_WIDTH))
    g, u = jnp.split(c, 2, axis=-1)
    return (jax.nn.silu(g) * u) @ w_down


def per_layer_embed(h, p_i, g_norm, w_in, w_gate, b_gate):
    gate = jax.nn.sigmoid(rmsnorm(h, g_norm) @ w_gate + b_gate)
    return (p_i @ w_in) * gate


def setup_inputs(seed: int = 0) -> dict:
    key = jax.random.key(seed)
    ks = jax.random.split(key, 26)
    f32 = jnp.float32

    def nrm(k, shape, scale):
        return jax.random.normal(k, shape, f32) * scale

    def gain(k, shape):
        return 1.0 + 0.02 * jax.random.normal(k, shape, f32)

    d = D_MODEL
    return {
        "x": nrm(ks[0], (BATCH, SEQ, d), 1.0),
        "p": nrm(ks[1], (DEPTH, BATCH, SEQ, PLE_DIM), 1.0),
        "norm_mix": gain(ks[2], (DEPTH, d)),
        "norm_ffn": gain(ks[3], (DEPTH, d)),
        "norm_ple": gain(ks[4], (DEPTH, d)),
        "norm_kv": gain(ks[5], (d,)),
        "norm_final": gain(ks[6], (d,)),
        "a_w_in": nrm(ks[7], (N_A_LAYERS, d, 2 * A_WIDTH), d ** -0.5),
        "a_norm_v": gain(ks[8], (N_A_LAYERS, A_WIDTH)),
        "a_w_s": nrm(ks[9], (N_A_LAYERS, A_GROUPS, CHUNK, CHUNK), CHUNK ** -0.5),
        "a_b_s": 1.0 + nrm(ks[10], (N_A_LAYERS, A_GROUPS, CHUNK), 0.02),
        "a_w_out": nrm(ks[11], (N_A_LAYERS, A_WIDTH, d), A_WIDTH ** -0.5),
        "w_kv": nrm(ks[12], (d, 2 * N_KV_HEADS * HEAD_DIM), d ** -0.5),
        "b_w_q": nrm(ks[13], (N_B_LAYERS, d, N_Q_HEADS * HEAD_DIM), d ** -0.5),
        "b_sinks": nrm(ks[14], (N_B_LAYERS, N_Q_HEADS), 0.5),
        "b_w_o": nrm(ks[15], (N_B_LAYERS, N_Q_HEADS * HEAD_DIM, d), (N_Q_HEADS * HEAD_DIM) ** -0.5),
        "f_w_up": nrm(ks[16], (DEPTH, d, 2 * D_FF), d ** -0.5),
        "f_conv_w": nrm(ks[17], (DEPTH, CONV_WIDTH, 2 * D_FF), CONV_WIDTH ** -0.5),
        "f_conv_b": nrm(ks[18], (DEPTH, 2 * D_FF), 0.01),
        "f_w_down": nrm(ks[19], (DEPTH, D_FF, d), D_FF ** -0.5),
        "ple_w_in": nrm(ks[20], (DEPTH, PLE_DIM, d), PLE_DIM ** -0.5),
        "ple_w_gate": nrm(ks[21], (DEPTH, d, d), d ** -0.5),
        "ple_b_gate": nrm(ks[22], (DEPTH, d), 0.01),
    }


def reference(x, p, norm_mix, norm_ffn, norm_ple, norm_kv, norm_final,
              a_w_in, a_norm_v, a_w_s, a_b_s, a_w_out,
              w_kv, b_w_q, b_sinks, b_w_o,
              f_w_up, f_conv_w, f_conv_b, f_w_down,
              ple_w_in, ple_w_gate, ple_b_gate):
    h = x
    kblk = None
    vblk = None
    for i in range(DEPTH):
        xn = rmsnorm(h, norm_mix[i])
        if i < N_A_LAYERS:
            a = i
            h = h + mixer_a(xn, a_w_in[a], a_norm_v[a], a_w_s[a], a_b_s[a], a_w_out[a])
        else:
            b = i - N_A_LAYERS
            h = h + mixer_b(xn, b_w_q[b], b_sinks[b], b_w_o[b], kblk, vblk)
        h = h + conv_ffn(rmsnorm(h, norm_ffn[i]), f_w_up[i], f_conv_w[i], f_conv_b[i], f_w_down[i])
        h = h + per_layer_embed(h, p[i], norm_ple[i], ple_w_in[i], ple_w_gate[i], ple_b_gate[i])
        if i == N_A_LAYERS - 1:
            kblk, vblk = shared_kv(h, norm_kv, w_kv)
    return rmsnorm(h, norm_final)
```

```python
import functools
import math

import jax
import jax.numpy as jnp
import numpy as np
from jax import lax
from jax.experimental import pallas as pl
from jax.experimental.pallas import tpu as pltpu

D_MODEL = 1024
SEQ = 8192
CHUNK = 128
A_GROUPS = 8
HEAD_DIM = 64
N_Q_HEADS = 16
N_KV_HEADS = 4
GQA_GROUP = N_Q_HEADS // N_KV_HEADS
D_FF = 2816
CONV_WIDTH = 3
PLE_DIM = 256
EPS = 1e-6

LANES = 128
SUBLANES = 8
MXU_WIDTH = 256
V7X_VMEM_BYTES = 64 * 1024 * 1024

TM_A = 512
TM_F = 512
TM_B = 512
FF_CHUNK = MXU_WIDTH
N_FF_CHUNKS = D_FF // FF_CHUNK
CARRY_ROWS = SUBLANES

_BF16 = jnp.bfloat16
_F32 = jnp.float32


def _dot(a, b):
    return jnp.dot(a, b, preferred_element_type=_F32)


def _rms(x, g):
    return x * lax.rsqrt(jnp.mean(x * x, axis=-1, keepdims=True) + EPS) * g


def _const_spec(shape):
    zeros = (0,) * len(shape)
    return pl.BlockSpec(shape, lambda i: zeros, pipeline_mode=pl.Buffered(1))


def _row_spec(tm, width):
    return pl.BlockSpec((tm, width), lambda i: (i, 0))


def _nbytes(shape, dtype):
    return math.prod(shape) * jnp.dtype(dtype).itemsize


def _vmem_limit(resident, streamed, scratch, temporaries):
    total = sum(resident) + 2 * sum(streamed) + sum(scratch) + sum(temporaries)
    return min(int(total), V7X_VMEM_BYTES)


def _mixer_a_kernel(x_ref, g_ref, w_in_ref, gv_ref, ws_ref, bs_ref, w_out_ref, o_ref,
                    z_ref, v_ref, gated_ref):
    tm = x_ref.shape[0]
    x = x_ref[...]
    xn = _rms(x, g_ref[...]).astype(_BF16)
    z = _dot(xn, w_in_ref[...])
    z_ref[...] = jax.nn.gelu(z, approximate=True)
    v_ref[...] = _rms(z_ref[:, D_MODEL:], gv_ref[...]).astype(_BF16)

    row = lax.broadcasted_iota(jnp.int32, (CHUNK, CHUNK), 0)
    col = lax.broadcasted_iota(jnp.int32, (CHUNK, CHUNK), 1)
    causal = row >= col
    for h in range(A_GROUPS):
        w_h = jnp.where(causal, ws_ref[h], 0.0).astype(_BF16)
        lanes = slice(h * LANES, (h + 1) * LANES)
        for c in range(tm // CHUNK):
            rows = slice(c * CHUNK, (c + 1) * CHUNK)
            s = _dot(w_h, v_ref[rows, lanes]) + bs_ref[:, lanes]
            gated_ref[rows, lanes] = (z_ref[rows, lanes] * s).astype(_BF16)
    o_ref[...] = x + _dot(gated_ref[...], w_out_ref[...])


def _mixer_a(x, g, w_in, gv, ws, bs, w_out):
    m = x.shape[0]
    tm = TM_A
    resident = [_nbytes(w_in.shape, _BF16), _nbytes(w_out.shape, _BF16), _nbytes(ws.shape, _F32),
                _nbytes(bs.shape, _F32)]
    streamed = [2 * _nbytes((tm, D_MODEL), _F32)]
    scratch = [_nbytes((tm, 2 * D_MODEL), _F32), 2 * _nbytes((tm, D_MODEL), _BF16)]
    temporaries = [2 * _nbytes((tm, 2 * D_MODEL), _F32)]
    return pl.pallas_call(
        _mixer_a_kernel,
        out_shape=jax.ShapeDtypeStruct((m, D_MODEL), _F32),
        grid=(m // tm,),
        in_specs=[
            _row_spec(tm, D_MODEL),
            _const_spec(g.shape),
            _const_spec(w_in.shape),
            _const_spec(gv.shape),
            _const_spec(ws.shape),
            _const_spec(bs.shape),
            _const_spec(w_out.shape),
        ],
        out_specs=_row_spec(tm, D_MODEL),
        scratch_shapes=[
            pltpu.VMEM((tm, 2 * D_MODEL), _F32),
            pltpu.VMEM((tm, D_MODEL), _BF16),
            pltpu.VMEM((tm, D_MODEL), _BF16),
        ],
        compiler_params=pltpu.CompilerParams(
            dimension_semantics=("arbitrary",),
            vmem_limit_bytes=_vmem_limit(resident, streamed, scratch, temporaries)),
        name="mixer_a",
    )(x, g, w_in, gv, ws, bs, w_out)


def _ffn_ple_kernel(final_norm, h_ref, p_ref, gf_ref, wup_ref, cw_ref, cb_ref, wdown_ref,
                    gp_ref, wpi_ref, wg_ref, bg_ref, ge_ref, *rest):
    if final_norm:
        o_ref, hn_ref, gu_ref, carry_ref, act_ref = rest
    else:
        we_ref, o_ref, kv_ref, hn_ref, gu_ref, carry_ref, act_ref = rest
    tm = h_ref.shape[0]
    first_of_seq = (pl.program_id(0) % (SEQ // tm)) == 0

    h = h_ref[...]
    hn_ref[...] = _rms(h, gf_ref[...]).astype(_BF16)
    for c in range(N_FF_CHUNKS):
        buf = c % 2
        gu_ref[buf, :CARRY_ROWS, :] = jnp.where(first_of_seq, 0.0, carry_ref[c])
        gu_ref[buf, CARRY_ROWS:, :] = _dot(hn_ref[...], wup_ref[c])
        carry_ref[c] = gu_ref[buf, tm:, :]
        conv = cb_ref[c]
        for t in range(CONV_WIDTH):
            shift = CONV_WIDTH - 1 - t
            conv = conv + gu_ref[buf, pl.ds(CARRY_ROWS - shift, tm), :] * cw_ref[c, t:t + 1, :]
        gate = conv[:, :FF_CHUNK]
        val = conv[:, FF_CHUNK:]
        act_ref[:, c * FF_CHUNK:(c + 1) * FF_CHUNK] = (jax.nn.silu(gate) * val).astype(_BF16)
    h = h + _dot(act_ref[...], wdown_ref[...])

    hn = _rms(h, gp_ref[...]).astype(_BF16)
    gate = jax.nn.sigmoid(_dot(hn, wg_ref[...]) + bg_ref[...])
    h = h + _dot(p_ref[...].astype(_BF16), wpi_ref[...]) * gate

    if final_norm:
        o_ref[...] = _rms(h, ge_ref[...])
    else:
        o_ref[...] = h
        kv_ref[...] = _dot(_rms(h, ge_ref[...]).astype(_BF16), we_ref[...]).astype(_BF16)


def _ffn_ple(h, p, gf, wup, cw, cb, wdown, gp, wpi, wg, bg, ge, we=None, *, name):
    m = h.shape[0]
    tm = TM_F
    final_norm = we is None
    weights = (gf, wup, cw, cb, wdown, gp, wpi, wg, bg, ge) + (() if final_norm else (we,))
    out_shape = [jax.ShapeDtypeStruct((m, D_MODEL), _F32)]
    out_specs = [_row_spec(tm, D_MODEL)]
    streamed = [2 * _nbytes((tm, D_MODEL), _F32), _nbytes((tm, PLE_DIM), _F32)]
    if not final_norm:
        out_shape.append(jax.ShapeDtypeStruct((m, we.shape[1]), _BF16))
        out_specs.append(_row_spec(tm, we.shape[1]))
        streamed.append(_nbytes((tm, we.shape[1]), _BF16))
    scratch_shapes = [
        pltpu.VMEM((tm, D_MODEL), _BF16),
        pltpu.VMEM((2, tm + CARRY_ROWS, 2 * FF_CHUNK), _F32),
        pltpu.VMEM((N_FF_CHUNKS, CARRY_ROWS, 2 * FF_CHUNK), _F32),
        pltpu.VMEM((tm, D_FF), _BF16),
    ]
    resident = [_nbytes(w.shape, w.dtype) for w in weights]
    scratch = [_nbytes(s.shape, s.dtype) for s in scratch_shapes]
    temporaries = [4 * _nbytes((tm, D_MODEL), _F32), 2 * _nbytes((tm, 2 * FF_CHUNK), _F32)]
    outs = pl.pallas_call(
        functools.partial(_ffn_ple_kernel, final_norm),
        out_shape=out_shape,
        grid=(m // tm,),
        in_specs=[_row_spec(tm, D_MODEL), _row_spec(tm, PLE_DIM)] + [_const_spec(w.shape) for w in weights],
        out_specs=out_specs,
        scratch_shapes=scratch_shapes,
        compiler_params=pltpu.CompilerParams(
            dimension_semantics=("arbitrary",),
            vmem_limit_bytes=_vmem_limit(resident, streamed, scratch, temporaries)),
        name=name,
    )(h, p, *weights)
    return outs[0] if final_norm else outs


def _head_permutation():
    cols = []
    for j in range(N_Q_HEADS // 2):
        for half in range(2):
            head = (2 * (j // GQA_GROUP) + half) * GQA_GROUP + j % GQA_GROUP
            cols.extend(range(head * HEAD_DIM, (head + 1) * HEAD_DIM))
    return np.asarray(cols, dtype=np.int32)


def _attention_bias():
    i = np.arange(CHUNK)[:, None]
    j = np.arange(2 * CHUNK)[None, :]
    dist = (i + CHUNK - j).astype(np.float32)
    in_band = (dist >= 0) & (dist < CHUNK)
    slopes = np.array([2.0 ** (-8.0 * (h + 1) / N_Q_HEADS) for h in range(N_Q_HEADS)], dtype=np.float32)
    bias = -slopes[:, None, None] * dist[None]
    general = np.where(in_band[None], bias, -np.inf)
    first = np.where((in_band & (j >= CHUNK))[None], bias, -np.inf)
    return np.stack([general, first]).astype(np.float32)


def _mixer_b_kernel(sinks_ref, h_ref, kvc_ref, kvp_ref, g_ref, wq_ref, wo_ref, bias_ref, o_ref,
                    q_ref, att_ref):
    tm = h_ref.shape[0]
    blocks_per_seq = SEQ // CHUNK
    blocks_per_tile = tm // CHUNK
    kv_width = N_KV_HEADS * HEAD_DIM

    h = h_ref[...]
    q = _dot(_rms(h, g_ref[...]).astype(_BF16), wq_ref[...]) * (HEAD_DIM ** -0.5)
    q_ref[...] = q.astype(_BF16)

    lane = lax.broadcasted_iota(jnp.int32, (2 * CHUNK, LANES), 1)
    low_half = lane < HEAD_DIM
    for b in range(blocks_per_tile):
        rows = slice(b * CHUNK, (b + 1) * CHUNK)
        block = pl.program_id(0) * blocks_per_tile + b
        first = jnp.where(block % blocks_per_seq == 0, 1, 0)
        if b == 0:
            kv_prev = kvp_ref[...]
        else:
            kv_prev = kvc_ref[(b - 1) * CHUNK:b * CHUNK, :]
        kv = jnp.concatenate([kv_prev, kvc_ref[rows, :]], axis=0)
        for pair in range(N_KV_HEADS // 2):
            k_pair = kv[:, pair * LANES:(pair + 1) * LANES]
            v_pair = kv[:, kv_width + pair * LANES:kv_width + (pair + 1) * LANES]
            groups = [pair * GQA_GROUP + g for g in range(GQA_GROUP)]
            q_stack = jnp.concatenate([q_ref[rows, j * LANES:(j + 1) * LANES] for j in groups], axis=0)
            probs = []
            for half in range(2):
                kv_head = 2 * pair + half
                k_half = jnp.where(low_half if half == 0 else ~low_half, k_pair, 0)
                s = lax.dot_general(q_stack, k_half, (((1,), (1,)), ((), ())),
                                    preferred_element_type=_F32)
                parts = []
                for g in range(GQA_GROUP):
                    head = kv_head * GQA_GROUP + g
                    sg = s[g * CHUNK:(g + 1) * CHUNK, :] + bias_ref[first, head]
                    sink = sinks_ref[head]
                    mx = jnp.maximum(jnp.max(sg, axis=-1, keepdims=True), sink)
                    e = jnp.exp(sg - mx)
                    denom = jnp.sum(e, axis=-1, keepdims=True) + jnp.exp(sink - mx)
                    parts.append((e * (1.0 / denom)).astype(_BF16))
                probs.append(jnp.concatenate(parts, axis=0))
            p_cat = jnp.concatenate(probs, axis=1)
            v_cat = jnp.concatenate([jnp.where(low_half, v_pair, 0), jnp.where(low_half, 0, v_pair)], axis=0)
            out = _dot(p_cat, v_cat)
            for g in range(GQA_GROUP):
                j = groups[g]
                att_ref[rows, j * LANES:(j + 1) * LANES] = out[g * CHUNK:(g + 1) * CHUNK, :].astype(_BF16)
    o_ref[...] = h + _dot(att_ref[...], wo_ref[...])


def _mixer_b(h, kv, g, wq, wo, sinks, bias):
    m = h.shape[0]
    tm = TM_B
    blocks_per_tile = tm // CHUNK
    kvw = kv.shape[1]
    resident = [_nbytes(wq.shape, _BF16), _nbytes(wo.shape, _BF16), _nbytes(bias.shape, _F32)]
    streamed = [2 * _nbytes((tm, D_MODEL), _F32), _nbytes((tm + CHUNK, kvw), _BF16)]
    scratch = [2 * _nbytes((tm, D_MODEL), _BF16)]
    temporaries = [3 * _nbytes((tm, D_MODEL), _F32), 8 * _nbytes((GQA_GROUP * CHUNK, 2 * CHUNK), _F32)]

    return pl.pallas_call(
        _mixer_b_kernel,
        out_shape=jax.ShapeDtypeStruct((m, D_MODEL), _F32),
        grid=(m // tm,),
        in_specs=[
            pl.BlockSpec(memory_space=pltpu.SMEM),
            _row_spec(tm, D_MODEL),
            _row_spec(tm, kvw),
            pl.BlockSpec((CHUNK, kvw), lambda i: (jnp.maximum(i * blocks_per_tile - 1, 0), 0)),
            _const_spec(g.shape),
            _const_spec(wq.shape),
            _const_spec(wo.shape),
            _const_spec(bias.shape),
        ],
        out_specs=_row_spec(tm, D_MODEL),
        scratch_shapes=[pltpu.VMEM((tm, D_MODEL), _BF16), pltpu.VMEM((tm, D_MODEL), _BF16)],
        compiler_params=pltpu.CompilerParams(
            dimension_semantics=("arbitrary",),
            vmem_limit_bytes=_vmem_limit(resident, streamed, scratch, temporaries)),
        name="mixer_b",
    )(sinks, h, kv, kv, g, wq, wo, bias)


def _ffn_weights(layer, f_w_up, f_conv_w, f_conv_b, f_w_down):
    def chunked(a):
        lead = a.shape[:-1]
        a = a.reshape(lead + (2, N_FF_CHUNKS, FF_CHUNK))
        a = jnp.moveaxis(a, -2, 0)
        return a.reshape((N_FF_CHUNKS,) + lead + (2 * FF_CHUNK,))
    wup = chunked(f_w_up[layer]).astype(_BF16)
    cw = chunked(f_conv_w[layer])
    cb = chunked(f_conv_b[layer][None, :])
    return wup, cw, cb, f_w_down[layer].astype(_BF16)


def kernel(x, p, norm_mix, norm_ffn, norm_ple, norm_kv, norm_final, a_w_in, a_norm_v, a_w_s, a_b_s, a_w_out, w_kv, b_w_q, b_sinks, b_w_o, f_w_up, f_conv_w, f_conv_b, f_w_down, ple_w_in, ple_w_gate, ple_b_gate):
    batch, seq, d = x.shape
    assert (seq, d) == (SEQ, D_MODEL) and seq % max(TM_A, TM_F, TM_B) == 0
    m = batch * seq
    row = lambda a: a.reshape(1, -1)

    h = x.reshape(m, d)
    bias_s = jnp.repeat(a_b_s[0].T, LANES, axis=1)
    h = _mixer_a(h, row(norm_mix[0]), a_w_in[0].astype(_BF16), row(a_norm_v[0]), a_w_s[0], bias_s,
                 a_w_out[0].astype(_BF16))

    def ffn_ple(h, layer, ge, we, name):
        wup, cw, cb, wdown = _ffn_weights(layer, f_w_up, f_conv_w, f_conv_b, f_w_down)
        return _ffn_ple(h, p[layer].reshape(m, PLE_DIM), row(norm_ffn[layer]), wup, cw, cb, wdown,
                        row(norm_ple[layer]), ple_w_in[layer].astype(_BF16), ple_w_gate[layer].astype(_BF16),
                        row(ple_b_gate[layer]), row(ge), we, name=name)

    h, kv = ffn_ple(h, 0, norm_kv, w_kv.astype(_BF16), "ffn_ple_0")

    perm = _head_permutation()
    h = _mixer_b(h, kv, row(norm_mix[1]), b_w_q[0][:, perm].astype(_BF16), b_w_o[0][perm, :].astype(_BF16),
                 b_sinks[0], jnp.asarray(_attention_bias()))

    out = ffn_ple(h, 1, norm_final, None, "ffn_ple_1")
    return out.reshape(batch, seq, d)
```

```python
import functools
import math

import jax
import jax.numpy as jnp
import numpy as np
from jax import lax
from jax.experimental import pallas as pl
from jax.experimental.pallas import tpu as pltpu

D_MODEL = 1024
SEQ = 8192
CHUNK = 128
A_GROUPS = 8
HEAD_DIM = 64
N_Q_HEADS = 16
N_KV_HEADS = 4
GQA_GROUP = N_Q_HEADS // N_KV_HEADS
D_FF = 2816
CONV_WIDTH = 3
PLE_DIM = 256
EPS = 1e-6

LANES = 128
SUBLANES = 8
MXU_WIDTH = 256
V7X_VMEM_BYTES = 64 * 1024 * 1024

TM_A = 512
TM_F = 512
TM_B = 512
FF_CHUNK = MXU_WIDTH
N_FF_CHUNKS = D_FF // FF_CHUNK
GROUP_TILES = CHUNK // SUBLANES
GROUP = SUBLANES * GROUP_TILES
CARRY_ROWS = (CONV_WIDTH - 1) * SUBLANES

_BF16 = jnp.bfloat16
_F32 = jnp.float32


def _dot(a, b):
    return jnp.dot(a, b, preferred_element_type=_F32)


def _rms(x, g):
    return x * lax.rsqrt(jnp.mean(x * x, axis=-1, keepdims=True) + EPS) * g


def _const_spec(shape):
    zeros = (0,) * len(shape)
    return pl.BlockSpec(shape, lambda i: zeros, pipeline_mode=pl.Buffered(1))


def _row_spec(tm, width):
    return pl.BlockSpec((tm, width), lambda i: (i, 0))


def _nbytes(shape, dtype):
    return math.prod(shape) * jnp.dtype(dtype).itemsize


def _vmem_limit(resident, streamed, scratch, temporaries):
    total = sum(resident) + 2 * sum(streamed) + sum(scratch) + sum(temporaries)
    return min(int(total), V7X_VMEM_BYTES)


def _mixer_a_kernel(x_ref, g_ref, w_in_ref, gv_ref, ws_ref, bs_ref, w_out_ref, o_ref,
                    z_ref, v_ref, gated_ref):
    tm = x_ref.shape[0]
    x = x_ref[...]
    xn = _rms(x, g_ref[...]).astype(_BF16)
    z = _dot(xn, w_in_ref[...])
    z_ref[...] = jax.nn.gelu(z, approximate=True)
    v_ref[...] = _rms(z_ref[:, D_MODEL:], gv_ref[...]).astype(_BF16)

    row = lax.broadcasted_iota(jnp.int32, (CHUNK, CHUNK), 0)
    col = lax.broadcasted_iota(jnp.int32, (CHUNK, CHUNK), 1)
    causal = row >= col
    for h in range(A_GROUPS):
        w_h = jnp.where(causal, ws_ref[h], 0.0).astype(_BF16)
        lanes = slice(h * LANES, (h + 1) * LANES)
        for c in range(tm // CHUNK):
            rows = slice(c * CHUNK, (c + 1) * CHUNK)
            s = _dot(w_h, v_ref[rows, lanes]) + bs_ref[:, lanes]
            gated_ref[rows, lanes] = (z_ref[rows, lanes] * s).astype(_BF16)
    o_ref[...] = x + _dot(gated_ref[...], w_out_ref[...])


def _mixer_a(x, g, w_in, gv, ws, bs, w_out):
    m = x.shape[0]
    tm = TM_A
    resident = [_nbytes(w_in.shape, _BF16), _nbytes(w_out.shape, _BF16), _nbytes(ws.shape, _F32),
                _nbytes(bs.shape, _F32)]
    streamed = [2 * _nbytes((tm, D_MODEL), _F32)]
    scratch = [_nbytes((tm, 2 * D_MODEL), _F32), 2 * _nbytes((tm, D_MODEL), _BF16)]
    temporaries = [2 * _nbytes((tm, 2 * D_MODEL), _F32)]
    return pl.pallas_call(
        _mixer_a_kernel,
        out_shape=jax.ShapeDtypeStruct((m, D_MODEL), _F32),
        grid=(m // tm,),
        in_specs=[
            _row_spec(tm, D_MODEL),
            _const_spec(g.shape),
            _const_spec(w_in.shape),
            _const_spec(gv.shape),
            _const_spec(ws.shape),
            _const_spec(bs.shape),
            _const_spec(w_out.shape),
        ],
        out_specs=_row_spec(tm, D_MODEL),
        scratch_shapes=[
            pltpu.VMEM((tm, 2 * D_MODEL), _F32),
            pltpu.VMEM((tm, D_MODEL), _BF16),
            pltpu.VMEM((tm, D_MODEL), _BF16),
        ],
        compiler_params=pltpu.CompilerParams(
            dimension_semantics=("arbitrary",),
            vmem_limit_bytes=_vmem_limit(resident, streamed, scratch, temporaries)),
        name="mixer_a",
    )(x, g, w_in, gv, ws, bs, w_out)


def _ffn_ple_kernel(final_norm, h_ref, p_ref, gf_ref, wup_ref, cw_ref, cb_ref, wdown_ref,
                    gp_ref, wpi_ref, wg_ref, bg_ref, ge_ref, *rest):
    if final_norm:
        o_ref, slab_ref, hn_ref, carry_ref, act_ref = rest
    else:
        we_ref, o_ref, kv_ref, slab_ref, hn_ref, carry_ref, act_ref = rest
    tm = h_ref.shape[0]
    first_of_seq = (pl.program_id(0) % (SEQ // tm)) == 0
    n_groups = tm // GROUP
    n_slabs = D_MODEL // LANES
    lanes = lambda l: slice(l * LANES, (l + 1) * LANES)

    h = h_ref[...]
    hn = _rms(h, gf_ref[...])
    for j in range(n_slabs):
        slab_ref[j] = hn[:, lanes(j)]
    for j in range(n_slabs):
        for k in range(n_groups):
            for q in range(0, GROUP_TILES, 2):
                pair = [slab_ref[j, pl.ds(GROUP * k + q + d, SUBLANES, stride=GROUP_TILES), :] for d in range(2)]
                rows = slice(GROUP * k + SUBLANES * q, GROUP * k + SUBLANES * (q + 2))
                hn_ref[rows, lanes(j)] = jnp.concatenate(pair, axis=0).astype(_BF16)

    first_row = lax.broadcasted_iota(jnp.int32, (SUBLANES, LANES), 0) == 0
    for c in range(N_FF_CHUNKS):
        cols = [slice(half * D_FF + c * FF_CHUNK, half * D_FF + (c + 1) * FF_CHUNK) for half in range(2)]
        up = [_dot(hn_ref[...], wup_ref[:, cols[half]]) for half in range(2)]
        tail = [jnp.where(first_of_seq, 0.0, carry_ref[half, c]) for half in range(2)]
        for half in range(2):
            carry_ref[half, c] = up[half][tm - CARRY_ROWS:, :]

        def conv_block(half, j, k):
            blk = up[half][GROUP * k:GROUP * (k + 1), lanes(j)]
            prev = tail[half][:, lanes(j)] if k == 0 else up[half][GROUP * k - CARRY_ROWS:GROUP * k, lanes(j)]
            wl = slice(cols[half].start + j * LANES, cols[half].start + (j + 1) * LANES)
            edge2, edge1 = (
                jnp.where(first_row, pltpu.roll(prev[s:s + SUBLANES], 1, 0),
                          pltpu.roll(blk[GROUP - 2 * SUBLANES + s:GROUP - SUBLANES + s], 1, 0))
                for s in (0, SUBLANES))
            back1 = jnp.concatenate([edge1, blk[:GROUP - SUBLANES]], axis=0)
            back2 = jnp.concatenate([edge2, edge1, blk[:GROUP - 2 * SUBLANES]], axis=0)
            w0, w1, w2 = (cw_ref[t:t + 1, wl] for t in range(CONV_WIDTH))
            return cb_ref[:, wl] + back2 * w0 + back1 * w1 + blk * w2

        for j in range(FF_CHUNK // LANES):
            for k in range(n_groups):
                gate, val = conv_block(0, j, k), conv_block(1, j, k)
                act_ref[GROUP * k:GROUP * (k + 1), lanes(c * (FF_CHUNK // LANES) + j)] = (
                    jax.nn.silu(gate) * val).astype(_BF16)

    down = _dot(act_ref[...], wdown_ref[...])
    for j in range(n_slabs):
        for k in range(n_groups):
            for q in range(GROUP_TILES):
                rows = slice(GROUP * k + SUBLANES * q, GROUP * k + SUBLANES * (q + 1))
                slab_ref[j, pl.ds(GROUP * k + q, SUBLANES, stride=GROUP_TILES), :] = down[rows, lanes(j)]
    h = h + jnp.concatenate([slab_ref[j] for j in range(n_slabs)], axis=1)

    hn = _rms(h, gp_ref[...]).astype(_BF16)
    gate = jax.nn.sigmoid(_dot(hn, wg_ref[...]) + bg_ref[...])
    h = h + _dot(p_ref[...].astype(_BF16), wpi_ref[...]) * gate

    if final_norm:
        o_ref[...] = _rms(h, ge_ref[...])
    else:
        o_ref[...] = h
        kv_ref[...] = _dot(_rms(h, ge_ref[...]).astype(_BF16), we_ref[...]).astype(_BF16)


def _layer_spec(shape, layer):
    zeros = (0,) * (len(shape) - 1)
    return pl.BlockSpec((None,) + tuple(shape[1:]), lambda i: (layer,) + zeros, pipeline_mode=pl.Buffered(1))


def _ffn_ple(h, p, layer, stacked, ge, we=None, *, name):
    m = h.shape[0]
    tm = TM_F
    final_norm = we is None
    consts = (ge,) + (() if final_norm else (we,))
    out_shape = [jax.ShapeDtypeStruct((m, D_MODEL), _F32)]
    out_specs = [_row_spec(tm, D_MODEL)]
    streamed = [2 * _nbytes((tm, D_MODEL), _F32), _nbytes((tm, PLE_DIM), _F32)]
    if not final_norm:
        out_shape.append(jax.ShapeDtypeStruct((m, we.shape[1]), _BF16))
        out_specs.append(_row_spec(tm, we.shape[1]))
        streamed.append(_nbytes((tm, we.shape[1]), _BF16))
    scratch_shapes = [
        pltpu.VMEM((D_MODEL // LANES, tm, LANES), _F32),
        pltpu.VMEM((tm, D_MODEL), _BF16),
        pltpu.VMEM((2, N_FF_CHUNKS, CARRY_ROWS, FF_CHUNK), _F32),
        pltpu.VMEM((tm, D_FF), _BF16),
    ]
    resident = [_nbytes(w.shape[1:], w.dtype) for w in stacked] + [_nbytes(w.shape, w.dtype) for w in consts]
    scratch = [_nbytes(s.shape, s.dtype) for s in scratch_shapes]
    temporaries = [4 * _nbytes((tm, D_MODEL), _F32), 2 * _nbytes((tm, 2 * FF_CHUNK), _F32)]
    outs = pl.pallas_call(
        functools.partial(_ffn_ple_kernel, final_norm),
        out_shape=out_shape,
        grid=(m // tm,),
        in_specs=([_row_spec(tm, D_MODEL), pl.BlockSpec((None, tm, PLE_DIM), lambda i: (layer, i, 0))]
                  + [_layer_spec(w.shape, layer) for w in stacked] + [_const_spec(w.shape) for w in consts]),
        out_specs=out_specs,
        scratch_shapes=scratch_shapes,
        compiler_params=pltpu.CompilerParams(
            dimension_semantics=("arbitrary",),
            vmem_limit_bytes=_vmem_limit(resident, streamed, scratch, temporaries)),
        name=name,
    )(h, p, *stacked, *consts)
    return outs[0] if final_norm else outs


def _pair_heads(w, axis):
    shape = w.shape
    split = shape[:axis] + (N_KV_HEADS // 2, 2, GQA_GROUP, HEAD_DIM) + shape[axis + 1:]
    return jnp.swapaxes(w.reshape(split), axis + 1, axis + 2).reshape(shape)


def _attention_bias():
    i = np.arange(CHUNK)[:, None]
    j = np.arange(2 * CHUNK)[None, :]
    dist = (i + CHUNK - j).astype(np.float32)
    in_band = (dist >= 0) & (dist < CHUNK)
    slopes = np.array([2.0 ** (-8.0 * (h + 1) / N_Q_HEADS) for h in range(N_Q_HEADS)], dtype=np.float32)
    bias = -slopes[:, None, None] * dist[None]
    general = np.where(in_band[None], bias, -np.inf)
    first = np.where((in_band & (j >= CHUNK))[None], bias, -np.inf)
    return np.stack([general, first]).astype(np.float32)


def _mixer_b_kernel(sinks_ref, h_ref, kvc_ref, kvp_ref, g_ref, wq_ref, wo_ref, bias_ref, o_ref,
                    q_ref, att_ref):
    tm = h_ref.shape[0]
    blocks_per_seq = SEQ // CHUNK
    blocks_per_tile = tm // CHUNK
    kv_width = N_KV_HEADS * HEAD_DIM

    h = h_ref[...]
    q = _dot(_rms(h, g_ref[...]).astype(_BF16), wq_ref[...]) * (HEAD_DIM ** -0.5)
    q_ref[...] = q.astype(_BF16)

    lane = lax.broadcasted_iota(jnp.int32, (2 * CHUNK, LANES), 1)
    low_half = lane < HEAD_DIM
    for b in range(blocks_per_tile):
        rows = slice(b * CHUNK, (b + 1) * CHUNK)
        block = pl.program_id(0) * blocks_per_tile + b
        first = jnp.where(block % blocks_per_seq == 0, 1, 0)
        if b == 0:
            kv_prev = kvp_ref[...]
        else:
            kv_prev = kvc_ref[(b - 1) * CHUNK:b * CHUNK, :]
        kv = jnp.concatenate([kv_prev, kvc_ref[rows, :]], axis=0)
        for pair in range(N_KV_HEADS // 2):
            k_pair = kv[:, pair * LANES:(pair + 1) * LANES]
            v_pair = kv[:, kv_width + pair * LANES:kv_width + (pair + 1) * LANES]
            groups = [pair * GQA_GROUP + g for g in range(GQA_GROUP)]
            q_stack = jnp.concatenate([q_ref[rows, j * LANES:(j + 1) * LANES] for j in groups], axis=0)
            probs = []
            for half in range(2):
                kv_head = 2 * pair + half
                k_half = jnp.where(low_half if half == 0 else ~low_half, k_pair, 0)
                s = lax.dot_general(q_stack, k_half, (((1,), (1,)), ((), ())),
                                    preferred_element_type=_F32)
                parts = []
                for g in range(GQA_GROUP):
                    head = kv_head * GQA_GROUP + g
                    sg = s[g * CHUNK:(g + 1) * CHUNK, :] + bias_ref[first, head]
                    sink = sinks_ref[head]
                    mx = jnp.maximum(jnp.max(sg, axis=-1, keepdims=True), sink)
                    e = jnp.exp(sg - mx)
                    denom = jnp.sum(e, axis=-1, keepdims=True) + jnp.exp(sink - mx)
                    parts.append((e * (1.0 / denom)).astype(_BF16))
                probs.append(jnp.concatenate(parts, axis=0))
            p_cat = jnp.concatenate(probs, axis=1)
            v_cat = jnp.concatenate([jnp.where(low_half, v_pair, 0), jnp.where(low_half, 0, v_pair)], axis=0)
            out = _dot(p_cat, v_cat)
            for g in range(GQA_GROUP):
                j = groups[g]
                att_ref[rows, j * LANES:(j + 1) * LANES] = out[g * CHUNK:(g + 1) * CHUNK, :].astype(_BF16)
    o_ref[...] = h + _dot(att_ref[...], wo_ref[...])


def _mixer_b(h, kv, g, wq, wo, sinks, bias):
    m = h.shape[0]
    tm = TM_B
    blocks_per_tile = tm // CHUNK
    kvw = kv.shape[1]
    resident = [_nbytes(wq.shape, _BF16), _nbytes(wo.shape, _BF16), _nbytes(bias.shape, _F32)]
    streamed = [2 * _nbytes((tm, D_MODEL), _F32), _nbytes((tm + CHUNK, kvw), _BF16)]
    scratch = [2 * _nbytes((tm, D_MODEL), _BF16)]
    temporaries = [3 * _nbytes((tm, D_MODEL), _F32), 8 * _nbytes((GQA_GROUP * CHUNK, 2 * CHUNK), _F32)]

    return pl.pallas_call(
        _mixer_b_kernel,
        out_shape=jax.ShapeDtypeStruct((m, D_MODEL), _F32),
        grid=(m // tm,),
        in_specs=[
            pl.BlockSpec(memory_space=pltpu.SMEM),
            _row_spec(tm, D_MODEL),
            _row_spec(tm, kvw),
            pl.BlockSpec((CHUNK, kvw), lambda i: (jnp.maximum(i * blocks_per_tile - 1, 0), 0)),
            _const_spec(g.shape),
            _const_spec(wq.shape),
            _const_spec(wo.shape),
            _const_spec(bias.shape),
        ],
        out_specs=_row_spec(tm, D_MODEL),
        scratch_shapes=[pltpu.VMEM((tm, D_MODEL), _BF16), pltpu.VMEM((tm, D_MODEL), _BF16)],
        compiler_params=pltpu.CompilerParams(
            dimension_semantics=("arbitrary",),
            vmem_limit_bytes=_vmem_limit(resident, streamed, scratch, temporaries)),
        name="mixer_b",
    )(sinks, h, kv, kv, g, wq, wo, bias)


def kernel(x, p, norm_mix, norm_ffn, norm_ple, norm_kv, norm_final, a_w_in, a_norm_v, a_w_s, a_b_s, a_w_out, w_kv, b_w_q, b_sinks, b_w_o, f_w_up, f_conv_w, f_conv_b, f_w_down, ple_w_in, ple_w_gate, ple_b_gate):
    batch, seq, d = x.shape
    assert (seq, d) == (SEQ, D_MODEL) and seq % max(TM_A, TM_F, TM_B) == 0
    m = batch * seq
    row = lambda a: a.reshape(1, -1)

    h = x.reshape(m, d)
    bias_s = jnp.repeat(a_b_s[0].T, LANES, axis=1)
    h = _mixer_a(h, row(norm_mix[0]), a_w_in[0].astype(_BF16), row(a_norm_v[0]), a_w_s[0], bias_s,
                 a_w_out[0].astype(_BF16))

    rows = lambda a: a[:, None, :]
    stacked = (rows(norm_ffn), f_w_up.astype(_BF16), f_conv_w, rows(f_conv_b), f_w_down.astype(_BF16),
               rows(norm_ple), ple_w_in.astype(_BF16), ple_w_gate.astype(_BF16), rows(ple_b_gate))
    p = p.reshape(p.shape[0], m, PLE_DIM)

    h, kv = _ffn_ple(h, p, 0, stacked, row(norm_kv), w_kv.astype(_BF16), name="ffn_ple_0")

    h = _mixer_b(h, kv, row(norm_mix[1]), _pair_heads(b_w_q[0], 1).astype(_BF16),
                 _pair_heads(b_w_o[0], 0).astype(_BF16), b_sinks[0], jnp.asarray(_attention_bias()))

    out = _ffn_ple(h, p, 1, stacked, row(norm_final), name="ffn_ple_1")
    return out.reshape(batch, seq, d)
```

```python
import functools
import math

import jax
import jax.numpy as jnp
import numpy as np
from jax import lax
from jax.experimental import pallas as pl
from jax.experimental.pallas import tpu as pltpu

D_MODEL = 1024
SEQ = 8192
CHUNK = 128
A_GROUPS = 8
HEAD_DIM = 64
N_Q_HEADS = 16
N_KV_HEADS = 4
GQA_GROUP = N_Q_HEADS // N_KV_HEADS
D_FF = 2816
CONV_WIDTH = 3
PLE_DIM = 256
EPS = 1e-6

LANES = 128
SUBLANES = 8
MXU_WIDTH = 256
V7X_VMEM_BYTES = 64 * 1024 * 1024

TM_A = 1024
TM_F = 1024
TM_B = 1024
FF_CHUNK = MXU_WIDTH
N_FF_CHUNKS = D_FF // FF_CHUNK
GROUP_TILES = CHUNK // SUBLANES
GROUP = SUBLANES * GROUP_TILES
CARRY_ROWS = (CONV_WIDTH - 1) * SUBLANES

_BF16 = jnp.bfloat16
_F32 = jnp.float32


def _dot(a, b):
    return jnp.dot(a, b, preferred_element_type=_F32)


def _rms(x, g):
    return x * lax.rsqrt(jnp.mean(x * x, axis=-1, keepdims=True) + EPS) * g


def _const_spec(shape):
    zeros = (0,) * len(shape)
    return pl.BlockSpec(shape, lambda i: zeros, pipeline_mode=pl.Buffered(1))


def _row_spec(tm, width):
    return pl.BlockSpec((tm, width), lambda i: (i, 0))


def _nbytes(shape, dtype):
    return math.prod(shape) * jnp.dtype(dtype).itemsize


def _vmem_limit(resident, streamed, scratch, temporaries):
    total = sum(resident) + 2 * sum(streamed) + sum(scratch) + sum(temporaries)
    return min(int(total), V7X_VMEM_BYTES)


def _mixer_a_kernel(x_ref, g_ref, w_in_ref, gv_ref, ws_ref, bs_ref, w_out_ref, o_ref,
                    z_ref, v_ref, gated_ref):
    tm = x_ref.shape[0]
    x = x_ref[...]
    xn = _rms(x, g_ref[...]).astype(_BF16)
    z = _dot(xn, w_in_ref[...])
    z_ref[...] = jax.nn.gelu(z, approximate=True)
    v_ref[...] = _rms(z_ref[:, D_MODEL:], gv_ref[...]).astype(_BF16)

    row = lax.broadcasted_iota(jnp.int32, (CHUNK, CHUNK), 0)
    col = lax.broadcasted_iota(jnp.int32, (CHUNK, CHUNK), 1)
    causal = row >= col
    for h in range(A_GROUPS):
        w_h = jnp.where(causal, ws_ref[h], 0.0).astype(_BF16)
        lanes = slice(h * LANES, (h + 1) * LANES)
        for c in range(tm // CHUNK):
            rows = slice(c * CHUNK, (c + 1) * CHUNK)
            s = _dot(w_h, v_ref[rows, lanes]) + bs_ref[:, lanes]
            gated_ref[rows, lanes] = (z_ref[rows, lanes] * s).astype(_BF16)
    o_ref[...] = x + _dot(gated_ref[...], w_out_ref[...])


def _mixer_a(x, g, w_in, gv, ws, bs, w_out):
    m = x.shape[0]
    tm = TM_A
    resident = [_nbytes(w_in.shape, _BF16), _nbytes(w_out.shape, _BF16), _nbytes(ws.shape, _F32),
                _nbytes(bs.shape, _F32)]
    streamed = [2 * _nbytes((tm, D_MODEL), _F32)]
    scratch = [_nbytes((tm, 2 * D_MODEL), _F32), 2 * _nbytes((tm, D_MODEL), _BF16)]
    temporaries = [2 * _nbytes((tm, 2 * D_MODEL), _F32)]
    return pl.pallas_call(
        _mixer_a_kernel,
        out_shape=jax.ShapeDtypeStruct((m, D_MODEL), _F32),
        grid=(m // tm,),
        in_specs=[
            _row_spec(tm, D_MODEL),
            _const_spec(g.shape),
            _const_spec(w_in.shape),
            _const_spec(gv.shape),
            _const_spec(ws.shape),
            _const_spec(bs.shape),
            _const_spec(w_out.shape),
        ],
        out_specs=_row_spec(tm, D_MODEL),
        scratch_shapes=[
            pltpu.VMEM((tm, 2 * D_MODEL), _F32),
            pltpu.VMEM((tm, D_MODEL), _BF16),
            pltpu.VMEM((tm, D_MODEL), _BF16),
        ],
        compiler_params=pltpu.CompilerParams(
            dimension_semantics=("arbitrary",),
            vmem_limit_bytes=_vmem_limit(resident, streamed, scratch, temporaries)),
        name="mixer_a",
    )(x, g, w_in, gv, ws, bs, w_out)


def _ffn_ple_kernel(final_norm, h_ref, p_ref, gf_ref, wup_ref, cw_ref, cb_ref, wdown_ref,
                    gp_ref, wpi_ref, wg_ref, bg_ref, ge_ref, *rest):
    if final_norm:
        o_ref, slab_ref, hn_ref, carry_ref, act_ref = rest
    else:
        we_ref, o_ref, kv_ref, slab_ref, hn_ref, carry_ref, act_ref = rest
    tm = h_ref.shape[0]
    first_of_seq = (pl.program_id(0) % (SEQ // tm)) == 0
    n_groups = tm // GROUP
    n_slabs = D_MODEL // LANES
    lanes = lambda l: slice(l * LANES, (l + 1) * LANES)

    h = h_ref[...]
    hn = _rms(h, gf_ref[...])
    for j in range(n_slabs):
        slab_ref[j] = hn[:, lanes(j)]
    for j in range(n_slabs):
        for k in range(n_groups):
            for q in range(0, GROUP_TILES, 2):
                pair = [slab_ref[j, pl.ds(GROUP * k + q + d, SUBLANES, stride=GROUP_TILES), :] for d in range(2)]
                rows = slice(GROUP * k + SUBLANES * q, GROUP * k + SUBLANES * (q + 2))
                hn_ref[rows, lanes(j)] = jnp.concatenate(pair, axis=0).astype(_BF16)

    first_row = lax.broadcasted_iota(jnp.int32, (SUBLANES, LANES), 0) == 0
    for c in range(N_FF_CHUNKS):
        cols = [slice(half * D_FF + c * FF_CHUNK, half * D_FF + (c + 1) * FF_CHUNK) for half in range(2)]
        up = [_dot(hn_ref[...], wup_ref[:, cols[half]]) for half in range(2)]
        tail = [jnp.where(first_of_seq, 0.0, carry_ref[half, c]) for half in range(2)]
        for half in range(2):
            carry_ref[half, c] = up[half][tm - CARRY_ROWS:, :]

        def conv_block(half, j, k):
            blk = up[half][GROUP * k:GROUP * (k + 1), lanes(j)]
            prev = tail[half][:, lanes(j)] if k == 0 else up[half][GROUP * k - CARRY_ROWS:GROUP * k, lanes(j)]
            wl = slice(cols[half].start + j * LANES, cols[half].start + (j + 1) * LANES)
            edge2, edge1 = (
                jnp.where(first_row, pltpu.roll(prev[s:s + SUBLANES], 1, 0),
                          pltpu.roll(blk[GROUP - 2 * SUBLANES + s:GROUP - SUBLANES + s], 1, 0))
                for s in (0, SUBLANES))
            back1 = jnp.concatenate([edge1, blk[:GROUP - SUBLANES]], axis=0)
            back2 = jnp.concatenate([edge2, edge1, blk[:GROUP - 2 * SUBLANES]], axis=0)
            w0, w1, w2 = (cw_ref[t:t + 1, wl] for t in range(CONV_WIDTH))
            return cb_ref[:, wl] + back2 * w0 + back1 * w1 + blk * w2

        for j in range(FF_CHUNK // LANES):
            for k in range(n_groups):
                gate, val = conv_block(0, j, k), conv_block(1, j, k)
                act_ref[GROUP * k:GROUP * (k + 1), lanes(c * (FF_CHUNK // LANES) + j)] = (
                    jax.nn.silu(gate) * val).astype(_BF16)

    down = _dot(act_ref[...], wdown_ref[...])
    for j in range(n_slabs):
        for k in range(n_groups):
            for q in range(GROUP_TILES):
                rows = slice(GROUP * k + SUBLANES * q, GROUP * k + SUBLANES * (q + 1))
                slab_ref[j, pl.ds(GROUP * k + q, SUBLANES, stride=GROUP_TILES), :] = down[rows, lanes(j)]
    h = h + jnp.concatenate([slab_ref[j] for j in range(n_slabs)], axis=1)

    hn = _rms(h, gp_ref[...]).astype(_BF16)
    gate = jax.nn.sigmoid(_dot(hn, wg_ref[...]) + bg_ref[...])
    h = h + _dot(p_ref[...].astype(_BF16), wpi_ref[...]) * gate

    if final_norm:
        o_ref[...] = _rms(h, ge_ref[...])
    else:
        o_ref[...] = h
        kv_ref[...] = _dot(_rms(h, ge_ref[...]).astype(_BF16), we_ref[...]).astype(_BF16)


def _layer_spec(shape, layer):
    zeros = (0,) * (len(shape) - 1)
    return pl.BlockSpec((None,) + tuple(shape[1:]), lambda i: (layer,) + zeros, pipeline_mode=pl.Buffered(1))


def _ffn_ple(h, p, layer, stacked, ge, we=None, *, name):
    m = h.shape[0]
    tm = TM_F
    final_norm = we is None
    consts = (ge,) + (() if final_norm else (we,))
    out_shape = [jax.ShapeDtypeStruct((m, D_MODEL), _F32)]
    out_specs = [_row_spec(tm, D_MODEL)]
    streamed = [2 * _nbytes((tm, D_MODEL), _F32), _nbytes((tm, PLE_DIM), _F32)]
    if not final_norm:
        out_shape.append(jax.ShapeDtypeStruct((m, we.shape[1]), _BF16))
        out_specs.append(_row_spec(tm, we.shape[1]))
        streamed.append(_nbytes((tm, we.shape[1]), _BF16))
    scratch_shapes = [
        pltpu.VMEM((D_MODEL // LANES, tm, LANES), _F32),
        pltpu.VMEM((tm, D_MODEL), _BF16),
        pltpu.VMEM((2, N_FF_CHUNKS, CARRY_ROWS, FF_CHUNK), _F32),
        pltpu.VMEM((tm, D_FF), _BF16),
    ]
    resident = [_nbytes(w.shape[1:], w.dtype) for w in stacked] + [_nbytes(w.shape, w.dtype) for w in consts]
    scratch = [_nbytes(s.shape, s.dtype) for s in scratch_shapes]
    temporaries = [4 * _nbytes((tm, D_MODEL), _F32), 2 * _nbytes((tm, 2 * FF_CHUNK), _F32)]
    outs = pl.pallas_call(
        functools.partial(_ffn_ple_kernel, final_norm),
        out_shape=out_shape,
        grid=(m // tm,),
        in_specs=([_row_spec(tm, D_MODEL), pl.BlockSpec((None, tm, PLE_DIM), lambda i: (layer, i, 0))]
                  + [_layer_spec(w.shape, layer) for w in stacked] + [_const_spec(w.shape) for w in consts]),
        out_specs=out_specs,
        scratch_shapes=scratch_shapes,
        compiler_params=pltpu.CompilerParams(
            dimension_semantics=("arbitrary",),
            vmem_limit_bytes=_vmem_limit(resident, streamed, scratch, temporaries)),
        name=name,
    )(h, p, *stacked, *consts)
    return outs[0] if final_norm else outs


def _pair_heads(w, axis):
    shape = w.shape
    split = shape[:axis] + (N_KV_HEADS // 2, 2, GQA_GROUP, HEAD_DIM) + shape[axis + 1:]
    return jnp.swapaxes(w.reshape(split), axis + 1, axis + 2).reshape(shape)


def _attention_bias():
    i = np.arange(CHUNK)[:, None]
    j = np.arange(2 * CHUNK)[None, :]
    dist = (i + CHUNK - j).astype(np.float32)
    in_band = (dist >= 0) & (dist < CHUNK)
    slopes = np.array([2.0 ** (-8.0 * (h + 1) / N_Q_HEADS) for h in range(N_Q_HEADS)], dtype=np.float32)
    bias = -slopes[:, None, None] * dist[None]
    general = np.where(in_band[None], bias, -np.inf)
    first = np.where((in_band & (j >= CHUNK))[None], bias, -np.inf)
    return np.stack([general, first]).astype(np.float32)


def _mixer_b_kernel(sinks_ref, h_ref, kvc_ref, kvp_ref, g_ref, wq_ref, wo_ref, bias_ref, o_ref,
                    q_ref, att_ref):
    tm = h_ref.shape[0]
    blocks_per_seq = SEQ // CHUNK
    blocks_per_tile = tm // CHUNK
    kv_width = N_KV_HEADS * HEAD_DIM

    h = h_ref[...]
    q = _dot(_rms(h, g_ref[...]).astype(_BF16), wq_ref[...]) * (HEAD_DIM ** -0.5)
    q_ref[...] = q.astype(_BF16)

    lane = lax.broadcasted_iota(jnp.int32, (2 * CHUNK, LANES), 1)
    low_half = lane < HEAD_DIM
    for b in range(blocks_per_tile):
        rows = slice(b * CHUNK, (b + 1) * CHUNK)
        block = pl.program_id(0) * blocks_per_tile + b
        first = jnp.where(block % blocks_per_seq == 0, 1, 0)
        if b == 0:
            kv_prev = kvp_ref[...]
        else:
            kv_prev = kvc_ref[(b - 1) * CHUNK:b * CHUNK, :]
        kv = jnp.concatenate([kv_prev, kvc_ref[rows, :]], axis=0)
        for pair in range(N_KV_HEADS // 2):
            k_pair = kv[:, pair * LANES:(pair + 1) * LANES]
            v_pair = kv[:, kv_width + pair * LANES:kv_width + (pair + 1) * LANES]
            groups = [pair * GQA_GROUP + g for g in range(GQA_GROUP)]
            q_stack = jnp.concatenate([q_ref[rows, j * LANES:(j + 1) * LANES] for j in groups], axis=0)
            probs = []
            for half in range(2):
                kv_head = 2 * pair + half
                k_half = jnp.where(low_half if half == 0 else ~low_half, k_pair, 0)
                s = lax.dot_general(q_stack, k_half, (((1,), (1,)), ((), ())),
                                    preferred_element_type=_F32)
                parts = []
                for g in range(GQA_GROUP):
                    head = kv_head * GQA_GROUP + g
                    sg = s[g * CHUNK:(g + 1) * CHUNK, :] + bias_ref[first, head]
                    sink = sinks_ref[head]
                    mx = jnp.maximum(jnp.max(sg, axis=-1, keepdims=True), sink)
                    e = jnp.exp(sg - mx)
                    denom = jnp.sum(e, axis=-1, keepdims=True) + jnp.exp(sink - mx)
                    parts.append((e * (1.0 / denom)).astype(_BF16))
                probs.append(jnp.concatenate(parts, axis=0))
            p_cat = jnp.concatenate(probs, axis=1)
            v_cat = jnp.concatenate([jnp.where(low_half, v_pair, 0), jnp.where(low_half, 0, v_pair)], axis=0)
            out = _dot(p_cat, v_cat)
            for g in range(GQA_GROUP):
                j = groups[g]
                att_ref[rows, j * LANES:(j + 1) * LANES] = out[g * CHUNK:(g + 1) * CHUNK, :].astype(_BF16)
    o_ref[...] = h + _dot(att_ref[...], wo_ref[...])


def _mixer_b(h, kv, g, wq, wo, sinks, bias):
    m = h.shape[0]
    tm = TM_B
    blocks_per_tile = tm // CHUNK
    kvw = kv.shape[1]
    resident = [_nbytes(wq.shape, _BF16), _nbytes(wo.shape, _BF16), _nbytes(bias.shape, _F32)]
    streamed = [2 * _nbytes((tm, D_MODEL), _F32), _nbytes((tm + CHUNK, kvw), _BF16)]
    scratch = [2 * _nbytes((tm, D_MODEL), _BF16)]
    temporaries = [3 * _nbytes((tm, D_MODEL), _F32), 8 * _nbytes((GQA_GROUP * CHUNK, 2 * CHUNK), _F32)]

    return pl.pallas_call(
        _mixer_b_kernel,
        out_shape=jax.ShapeDtypeStruct((m, D_MODEL), _F32),
        grid=(m // tm,),
        in_specs=[
            pl.BlockSpec(memory_space=pltpu.SMEM),
            _row_spec(tm, D_MODEL),
            _row_spec(tm, kvw),
            pl.BlockSpec((CHUNK, kvw), lambda i: (jnp.maximum(i * blocks_per_tile - 1, 0), 0)),
            _const_spec(g.shape),
            _const_spec(wq.shape),
            _const_spec(wo.shape),
            _const_spec(bias.shape),
        ],
        out_specs=_row_spec(tm, D_MODEL),
        scratch_shapes=[pltpu.VMEM((tm, D_MODEL), _BF16), pltpu.VMEM((tm, D_MODEL), _BF16)],
        compiler_params=pltpu.CompilerParams(
            dimension_semantics=("arbitrary",),
            vmem_limit_bytes=_vmem_limit(resident, streamed, scratch, temporaries)),
        name="mixer_b",
    )(sinks, h, kv, kv, g, wq, wo, bias)


def kernel(x, p, norm_mix, norm_ffn, norm_ple, norm_kv, norm_final, a_w_in, a_norm_v, a_w_s, a_b_s, a_w_out, w_kv, b_w_q, b_sinks, b_w_o, f_w_up, f_conv_w, f_conv_b, f_w_down, ple_w_in, ple_w_gate, ple_b_gate):
    batch, seq, d = x.shape
    assert (seq, d) == (SEQ, D_MODEL) and seq % max(TM_A, TM_F, TM_B) == 0
    m = batch * seq
    row = lambda a: a.reshape(1, -1)

    h = x.reshape(m, d)
    bias_s = jnp.repeat(a_b_s[0].T, LANES, axis=1)
    h = _mixer_a(h, row(norm_mix[0]), a_w_in[0].astype(_BF16), row(a_norm_v[0]), a_w_s[0], bias_s,
                 a_w_out[0].astype(_BF16))

    rows = lambda a: a[:, None, :]
    stacked = (rows(norm_ffn), f_w_up.astype(_BF16), f_conv_w, rows(f_conv_b), f_w_down.astype(_BF16),
               rows(norm_ple), ple_w_in.astype(_BF16), ple_w_gate.astype(_BF16), rows(ple_b_gate))
    p = p.reshape(p.shape[0], m, PLE_DIM)

    h, kv = _ffn_ple(h, p, 0, stacked, row(norm_kv), w_kv.astype(_BF16), name="ffn_ple_0")

    h = _mixer_b(h, kv, row(norm_mix[1]), _pair_heads(b_w_q[0], 1).astype(_BF16),
                 _pair_heads(b_w_o[0], 0).astype(_BF16), b_sinks[0], jnp.asarray(_attention_bias()))

    out = _ffn_ple(h, p, 1, stacked, row(norm_final), name="ffn_ple_1")
    return out.reshape(batch, seq, d)
```

```python
import functools
import math

import jax
import jax.numpy as jnp
import numpy as np
from jax import lax
from jax.experimental import pallas as pl
from jax.experimental.pallas import tpu as pltpu

D_MODEL = 1024
SEQ = 8192
CHUNK = 128
A_GROUPS = 8
HEAD_DIM = 64
N_Q_HEADS = 16
N_KV_HEADS = 4
GQA_GROUP = N_Q_HEADS // N_KV_HEADS
D_FF = 2816
CONV_WIDTH = 3
PLE_DIM = 256
EPS = 1e-6

LANES = 128
SUBLANES = 8
MXU_WIDTH = 256
V7X_VMEM_BYTES = 64 * 1024 * 1024

TM_A = 1024
TM_F = 1024
TM_B = 1024
FF_CHUNK = MXU_WIDTH
N_FF_CHUNKS = D_FF // FF_CHUNK
GROUP_TILES = CHUNK // SUBLANES
GROUP = SUBLANES * GROUP_TILES
CARRY_ROWS = (CONV_WIDTH - 1) * SUBLANES

_BF16 = jnp.bfloat16
_F32 = jnp.float32


def _dot(a, b):
    return jnp.dot(a, b, preferred_element_type=_F32)


def _rms(x, g):
    return x * lax.rsqrt(jnp.mean(x * x, axis=-1, keepdims=True) + EPS) * g


def _const_spec(shape):
    zeros = (0,) * len(shape)
    return pl.BlockSpec(shape, lambda i: zeros, pipeline_mode=pl.Buffered(1))


def _row_spec(tm, width):
    return pl.BlockSpec((tm, width), lambda i: (i, 0))


def _nbytes(shape, dtype):
    return math.prod(shape) * jnp.dtype(dtype).itemsize


def _vmem_limit(resident, streamed, scratch, temporaries):
    total = sum(resident) + 2 * sum(streamed) + sum(scratch) + sum(temporaries)
    return min(int(total), V7X_VMEM_BYTES)


def _mixer_a_kernel(x_ref, g_ref, w_in_ref, gv_ref, ws_ref, bs_ref, w_out_ref, o_ref,
                    z_ref, v_ref, gated_ref):
    tm = x_ref.shape[0]
    x = x_ref[...]
    xn = _rms(x, g_ref[...]).astype(_BF16)
    z = _dot(xn, w_in_ref[...])
    z_ref[...] = jax.nn.gelu(z, approximate=True)
    v_ref[...] = _rms(z_ref[:, D_MODEL:], gv_ref[...]).astype(_BF16)

    row = lax.broadcasted_iota(jnp.int32, (CHUNK, CHUNK), 0)
    col = lax.broadcasted_iota(jnp.int32, (CHUNK, CHUNK), 1)
    causal = row >= col
    for h in range(A_GROUPS):
        w_h = jnp.where(causal, ws_ref[h], 0.0).astype(_BF16)
        lanes = slice(h * LANES, (h + 1) * LANES)
        for c in range(tm // CHUNK):
            rows = slice(c * CHUNK, (c + 1) * CHUNK)
            s = _dot(w_h, v_ref[rows, lanes]) + bs_ref[:, lanes]
            gated_ref[rows, lanes] = (z_ref[rows, lanes] * s).astype(_BF16)
    o_ref[...] = x + _dot(gated_ref[...], w_out_ref[...])


def _mixer_a(x, g, w_in, gv, ws, bs, w_out):
    m = x.shape[0]
    tm = TM_A
    resident = [_nbytes(w_in.shape, _BF16), _nbytes(w_out.shape, _BF16), _nbytes(ws.shape, _F32),
                _nbytes(bs.shape, _F32)]
    streamed = [2 * _nbytes((tm, D_MODEL), _F32)]
    scratch = [_nbytes((tm, 2 * D_MODEL), _F32), 2 * _nbytes((tm, D_MODEL), _BF16)]
    temporaries = [2 * _nbytes((tm, 2 * D_MODEL), _F32)]
    return pl.pallas_call(
        _mixer_a_kernel,
        out_shape=jax.ShapeDtypeStruct((m, D_MODEL), _F32),
        grid=(m // tm,),
        in_specs=[
            _row_spec(tm, D_MODEL),
            _const_spec(g.shape),
            _const_spec(w_in.shape),
            _const_spec(gv.shape),
            _const_spec(ws.shape),
            _const_spec(bs.shape),
            _const_spec(w_out.shape),
        ],
        out_specs=_row_spec(tm, D_MODEL),
        scratch_shapes=[
            pltpu.VMEM((tm, 2 * D_MODEL), _F32),
            pltpu.VMEM((tm, D_MODEL), _BF16),
            pltpu.VMEM((tm, D_MODEL), _BF16),
        ],
        compiler_params=pltpu.CompilerParams(
            dimension_semantics=("arbitrary",),
            vmem_limit_bytes=_vmem_limit(resident, streamed, scratch, temporaries)),
        name="mixer_a",
    )(x, g, w_in, gv, ws, bs, w_out)


def _ffn_ple_kernel(final_norm, h_ref, p_ref, gf_ref, wup_ref, cw_ref, cb_ref, wdown_ref,
                    gp_ref, wpi_ref, wg_ref, bg_ref, ge_ref, *rest):
    if final_norm:
        o_ref, slab_ref, hn_ref, carry_ref, act_ref = rest
    else:
        we_ref, o_ref, kv_ref, slab_ref, hn_ref, carry_ref, act_ref = rest
    tm = h_ref.shape[0]
    first_of_seq = (pl.program_id(0) % (SEQ // tm)) == 0
    n_groups = tm // GROUP
    n_slabs = D_MODEL // LANES
    chunk_tiles = FF_CHUNK // LANES
    lanes = lambda l: slice(l * LANES, (l + 1) * LANES)

    h = h_ref[...]
    hn = _rms(h, gf_ref[...])
    for j in range(n_slabs):
        slab_ref[j] = hn[:, lanes(j)]
    for j in range(n_slabs):
        for k in range(n_groups):
            for q in range(0, GROUP_TILES, 2):
                pair = [slab_ref[j, pl.ds(GROUP * k + q + d, SUBLANES, stride=GROUP_TILES), :] for d in range(2)]
                rows = slice(GROUP * k + SUBLANES * q, GROUP * k + SUBLANES * (q + 2))
                hn_ref[rows, lanes(j)] = jnp.concatenate(pair, axis=0).astype(_BF16)

    first_row = lax.broadcasted_iota(jnp.int32, (SUBLANES, LANES), 0) == 0
    tails = {}

    def conv_block(c, half, j, k):
        col = slab_ref.at[2 * chunk_tiles * (c % 2) + chunk_tiles * half + j]
        blk = col[GROUP * k:GROUP * (k + 1), :]
        prev = tails[c, half][:, lanes(j)] if k == 0 else col[GROUP * k - CARRY_ROWS:GROUP * k, :]
        start = half * D_FF + c * FF_CHUNK + j * LANES
        wl = slice(start, start + LANES)
        edge2, edge1 = (
            jnp.where(first_row, pltpu.roll(prev[s:s + SUBLANES], 1, 0),
                      pltpu.roll(blk[GROUP - 2 * SUBLANES + s:GROUP - SUBLANES + s], 1, 0))
            for s in (0, SUBLANES))
        back1 = jnp.concatenate([edge1, blk[:GROUP - SUBLANES]], axis=0)
        back2 = jnp.concatenate([edge2, edge1, blk[:GROUP - 2 * SUBLANES]], axis=0)
        w0, w1, w2 = (cw_ref[t:t + 1, wl] for t in range(CONV_WIDTH))
        return cb_ref[:, wl] + back2 * w0 + back1 * w1 + blk * w2

    for c in range(N_FF_CHUNKS + 1):
        if c < N_FF_CHUNKS:
            for half in range(2):
                start = half * D_FF + c * FF_CHUNK
                up = _dot(hn_ref[...], wup_ref[:, start:start + FF_CHUNK])
                tails[c, half] = jnp.where(first_of_seq, 0.0, carry_ref[half, c])
                carry_ref[half, c] = up[tm - CARRY_ROWS:, :]
                for j in range(chunk_tiles):
                    slab_ref[2 * chunk_tiles * (c % 2) + chunk_tiles * half + j] = up[:, lanes(j)]
        if c >= 1:
            for j in range(chunk_tiles):
                for k in range(n_groups):
                    act_ref[GROUP * k:GROUP * (k + 1), lanes((c - 1) * chunk_tiles + j)] = (
                        jax.nn.silu(conv_block(c - 1, 0, j, k)) * conv_block(c - 1, 1, j, k)).astype(_BF16)

    down = _dot(act_ref[...], wdown_ref[...])
    for j in range(n_slabs):
        for k in range(n_groups):
            for q in range(GROUP_TILES):
                rows = slice(GROUP * k + SUBLANES * q, GROUP * k + SUBLANES * (q + 1))
                slab_ref[j, pl.ds(GROUP * k + q, SUBLANES, stride=GROUP_TILES), :] = down[rows, lanes(j)]
    h = h + jnp.concatenate([slab_ref[j] for j in range(n_slabs)], axis=1)

    hn = _rms(h, gp_ref[...]).astype(_BF16)
    gate = jax.nn.sigmoid(_dot(hn, wg_ref[...]) + bg_ref[...])
    h = h + _dot(p_ref[...].astype(_BF16), wpi_ref[...]) * gate

    if final_norm:
        o_ref[...] = _rms(h, ge_ref[...])
    else:
        o_ref[...] = h
        kv_ref[...] = _dot(_rms(h, ge_ref[...]).astype(_BF16), we_ref[...]).astype(_BF16)


def _layer_spec(shape, layer):
    zeros = (0,) * (len(shape) - 1)
    return pl.BlockSpec((None,) + tuple(shape[1:]), lambda i: (layer,) + zeros, pipeline_mode=pl.Buffered(1))


def _ffn_ple(h, p, layer, stacked, ge, we=None, *, name):
    m = h.shape[0]
    tm = TM_F
    final_norm = we is None
    consts = (ge,) + (() if final_norm else (we,))
    out_shape = [jax.ShapeDtypeStruct((m, D_MODEL), _F32)]
    out_specs = [_row_spec(tm, D_MODEL)]
    streamed = [2 * _nbytes((tm, D_MODEL), _F32), _nbytes((tm, PLE_DIM), _F32)]
    if not final_norm:
        out_shape.append(jax.ShapeDtypeStruct((m, we.shape[1]), _BF16))
        out_specs.append(_row_spec(tm, we.shape[1]))
        streamed.append(_nbytes((tm, we.shape[1]), _BF16))
    scratch_shapes = [
        pltpu.VMEM((D_MODEL // LANES, tm, LANES), _F32),
        pltpu.VMEM((tm, D_MODEL), _BF16),
        pltpu.VMEM((2, N_FF_CHUNKS, CARRY_ROWS, FF_CHUNK), _F32),
        pltpu.VMEM((tm, D_FF), _BF16),
    ]
    resident = [_nbytes(w.shape[1:], w.dtype) for w in stacked] + [_nbytes(w.shape, w.dtype) for w in consts]
    scratch = [_nbytes(s.shape, s.dtype) for s in scratch_shapes]
    temporaries = [4 * _nbytes((tm, D_MODEL), _F32), 2 * _nbytes((tm, 2 * FF_CHUNK), _F32)]
    outs = pl.pallas_call(
        functools.partial(_ffn_ple_kernel, final_norm),
        out_shape=out_shape,
        grid=(m // tm,),
        in_specs=([_row_spec(tm, D_MODEL), pl.BlockSpec((None, tm, PLE_DIM), lambda i: (layer, i, 0))]
                  + [_layer_spec(w.shape, layer) for w in stacked] + [_const_spec(w.shape) for w in consts]),
        out_specs=out_specs,
        scratch_shapes=scratch_shapes,
        compiler_params=pltpu.CompilerParams(
            dimension_semantics=("arbitrary",),
            vmem_limit_bytes=_vmem_limit(resident, streamed, scratch, temporaries)),
        name=name,
    )(h, p, *stacked, *consts)
    return outs[0] if final_norm else outs


def _pair_heads(w, axis):
    shape = w.shape
    split = shape[:axis] + (N_KV_HEADS // 2, 2, GQA_GROUP, HEAD_DIM) + shape[axis + 1:]
    return jnp.swapaxes(w.reshape(split), axis + 1, axis + 2).reshape(shape)


def _attention_bias():
    i = np.arange(CHUNK)[:, None]
    j = np.arange(2 * CHUNK)[None, :]
    dist = (i + CHUNK - j).astype(np.float32)
    in_band = (dist >= 0) & (dist < CHUNK)
    slopes = np.array([2.0 ** (-8.0 * (h + 1) / N_Q_HEADS) for h in range(N_Q_HEADS)], dtype=np.float32)
    bias = -slopes[:, None, None] * dist[None]
    general = np.where(in_band[None], bias, -np.inf)
    first = np.where((in_band & (j >= CHUNK))[None], bias, -np.inf)
    return np.stack([general, first]).astype(np.float32)


def _mixer_b_kernel(sinks_ref, h_ref, kvc_ref, kvp_ref, g_ref, wq_ref, wo_ref, bias_ref, o_ref,
                    q_ref, att_ref):
    tm = h_ref.shape[0]
    blocks_per_seq = SEQ // CHUNK
    blocks_per_tile = tm // CHUNK
    kv_width = N_KV_HEADS * HEAD_DIM

    h = h_ref[...]
    q = _dot(_rms(h, g_ref[...]).astype(_BF16), wq_ref[...]) * (HEAD_DIM ** -0.5)
    q_ref[...] = q.astype(_BF16)

    lane = lax.broadcasted_iota(jnp.int32, (2 * CHUNK, LANES), 1)
    low_half = lane < HEAD_DIM
    for b in range(blocks_per_tile):
        rows = slice(b * CHUNK, (b + 1) * CHUNK)
        block = pl.program_id(0) * blocks_per_tile + b
        first = jnp.where(block % blocks_per_seq == 0, 1, 0)
        if b == 0:
            kv_prev = kvp_ref[...]
        else:
            kv_prev = kvc_ref[(b - 1) * CHUNK:b * CHUNK, :]
        kv = jnp.concatenate([kv_prev, kvc_ref[rows, :]], axis=0)
        for pair in range(N_KV_HEADS // 2):
            k_pair = kv[:, pair * LANES:(pair + 1) * LANES]
            v_pair = kv[:, kv_width + pair * LANES:kv_width + (pair + 1) * LANES]
            groups = [pair * GQA_GROUP + g for g in range(GQA_GROUP)]
            q_stack = jnp.concatenate([q_ref[rows, j * LANES:(j + 1) * LANES] for j in groups], axis=0)
            probs = []
            for half in range(2):
                kv_head = 2 * pair + half
                k_half = jnp.where(low_half if half == 0 else ~low_half, k_pair, 0)
                s = lax.dot_general(q_stack, k_half, (((1,), (1,)), ((), ())),
                                    preferred_element_type=_F32)
                parts = []
                for g in range(GQA_GROUP):
                    head = kv_head * GQA_GROUP + g
                    sg = s[g * CHUNK:(g + 1) * CHUNK, :] + bias_ref[first, head]
                    sink = sinks_ref[head]
                    mx = jnp.maximum(jnp.max(sg, axis=-1, keepdims=True), sink)
                    e = jnp.exp(sg - mx)
                    denom = jnp.sum(e, axis=-1, keepdims=True) + jnp.exp(sink - mx)
                    parts.append((e * (1.0 / denom)).astype(_BF16))
                probs.append(jnp.concatenate(parts, axis=0))
            p_cat = jnp.concatenate(probs, axis=1)
            v_cat = jnp.concatenate([jnp.where(low_half, v_pair, 0), jnp.where(low_half, 0, v_pair)], axis=0)
            out = _dot(p_cat, v_cat)
            for g in range(GQA_GROUP):
                j = groups[g]
                att_ref[rows, j * LANES:(j + 1) * LANES] = out[g * CHUNK:(g + 1) * CHUNK, :].astype(_BF16)
    o_ref[...] = h + _dot(att_ref[...], wo_ref[...])


def _mixer_b(h, kv, g, wq, wo, sinks, bias):
    m = h.shape[0]
    tm = TM_B
    blocks_per_tile = tm // CHUNK
    kvw = kv.shape[1]
    resident = [_nbytes(wq.shape, _BF16), _nbytes(wo.shape, _BF16), _nbytes(bias.shape, _F32)]
    streamed = [2 * _nbytes((tm, D_MODEL), _F32), _nbytes((tm + CHUNK, kvw), _BF16)]
    scratch = [2 * _nbytes((tm, D_MODEL), _BF16)]
    temporaries = [3 * _nbytes((tm, D_MODEL), _F32), 8 * _nbytes((GQA_GROUP * CHUNK, 2 * CHUNK), _F32)]

    return pl.pallas_call(
        _mixer_b_kernel,
        out_shape=jax.ShapeDtypeStruct((m, D_MODEL), _F32),
        grid=(m // tm,),
        in_specs=[
            pl.BlockSpec(memory_space=pltpu.SMEM),
            _row_spec(tm, D_MODEL),
            _row_spec(tm, kvw),
            pl.BlockSpec((CHUNK, kvw), lambda i: (jnp.maximum(i * blocks_per_tile - 1, 0), 0)),
            _const_spec(g.shape),
            _const_spec(wq.shape),
            _const_spec(wo.shape),
            _const_spec(bias.shape),
        ],
        out_specs=_row_spec(tm, D_MODEL),
        scratch_shapes=[pltpu.VMEM((tm, D_MODEL), _BF16), pltpu.VMEM((tm, D_MODEL), _BF16)],
        compiler_params=pltpu.CompilerParams(
            dimension_semantics=("arbitrary",),
            vmem_limit_bytes=_vmem_limit(resident, streamed, scratch, temporaries)),
        name="mixer_b",
    )(sinks, h, kv, kv, g, wq, wo, bias)


def kernel(x, p, norm_mix, norm_ffn, norm_ple, norm_kv, norm_final, a_w_in, a_norm_v, a_w_s, a_b_s, a_w_out, w_kv, b_w_q, b_sinks, b_w_o, f_w_up, f_conv_w, f_conv_b, f_w_down, ple_w_in, ple_w_gate, ple_b_gate):
    batch, seq, d = x.shape
    assert (seq, d) == (SEQ, D_MODEL) and seq % max(TM_A, TM_F, TM_B) == 0
    m = batch * seq
    row = lambda a: a.reshape(1, -1)

    h = x.reshape(m, d)
    bias_s = jnp.repeat(a_b_s[0].T, LANES, axis=1)
    h = _mixer_a(h, row(norm_mix[0]), a_w_in[0].astype(_BF16), row(a_norm_v[0]), a_w_s[0], bias_s,
                 a_w_out[0].astype(_BF16))

    rows = lambda a: a[:, None, :]
    stacked = (rows(norm_ffn), f_w_up.astype(_BF16), f_conv_w, rows(f_conv_b), f_w_down.astype(_BF16),
               rows(norm_ple), ple_w_in.astype(_BF16), ple_w_gate.astype(_BF16), rows(ple_b_gate))
    p = p.reshape(p.shape[0], m, PLE_DIM)

    h, kv = _ffn_ple(h, p, 0, stacked, row(norm_kv), w_kv.astype(_BF16), name="ffn_ple_0")

    h = _mixer_b(h, kv, row(norm_mix[1]), _pair_heads(b_w_q[0], 1).astype(_BF16),
                 _pair_heads(b_w_o[0], 0).astype(_BF16), b_sinks[0], jnp.asarray(_attention_bias()))

    out = _ffn_ple(h, p, 1, stacked, row(norm_final), name="ffn_ple_1")
    return out.reshape(batch, seq, d)
```

```python
import functools
import math

import jax
import jax.numpy as jnp
import numpy as np
from jax import lax
from jax.experimental import pallas as pl
from jax.experimental.pallas import tpu as pltpu

D_MODEL = 1024
SEQ = 8192
CHUNK = 128
A_GROUPS = 8
HEAD_DIM = 64
N_Q_HEADS = 16
N_KV_HEADS = 4
GQA_GROUP = N_Q_HEADS // N_KV_HEADS
D_FF = 2816
CONV_WIDTH = 3
PLE_DIM = 256
EPS = 1e-6

LANES = 128
SUBLANES = 8
MXU_WIDTH = 256
V7X_VMEM_BYTES = 64 * 1024 * 1024

TM_A = 1024
TM_F = 1024
TM_B = 1024
SUB_B = 256
FF_CHUNK = MXU_WIDTH
N_FF_CHUNKS = D_FF // FF_CHUNK
DOWN_SPLIT_CHUNKS = 8
GROUP_TILES = CHUNK // SUBLANES
GROUP = SUBLANES * GROUP_TILES
CARRY_ROWS = (CONV_WIDTH - 1) * SUBLANES

_BF16 = jnp.bfloat16
_F32 = jnp.float32


def _dot(a, b):
    return jnp.dot(a, b, preferred_element_type=_F32)


def _rms(x, g):
    return x * lax.rsqrt(jnp.mean(x * x, axis=-1, keepdims=True) + EPS) * g


def _const_spec(shape):
    zeros = (0,) * len(shape)
    return pl.BlockSpec(shape, lambda i: zeros, pipeline_mode=pl.Buffered(1))


def _row_spec(tm, width):
    return pl.BlockSpec((tm, width), lambda i: (i, 0))


def _nbytes(shape, dtype):
    return math.prod(shape) * jnp.dtype(dtype).itemsize


def _vmem_limit(resident, streamed, scratch, temporaries):
    total = sum(resident) + 2 * sum(streamed) + sum(scratch) + sum(temporaries)
    return min(int(total), V7X_VMEM_BYTES)


def _mixer_a_kernel(x_ref, g_ref, w_in_ref, gv_ref, ws_ref, bs_ref, w_out_ref, o_ref,
                    z_ref, v_ref, gated_ref):
    tm = x_ref.shape[0]
    x = x_ref[...]
    xn = _rms(x, g_ref[...]).astype(_BF16)
    z = _dot(xn, w_in_ref[...])
    z_ref[...] = jax.nn.gelu(z, approximate=True)
    v_ref[...] = _rms(z_ref[:, D_MODEL:], gv_ref[...]).astype(_BF16)

    row = lax.broadcasted_iota(jnp.int32, (CHUNK, CHUNK), 0)
    col = lax.broadcasted_iota(jnp.int32, (CHUNK, CHUNK), 1)
    causal = row >= col
    for h in range(A_GROUPS):
        w_h = jnp.where(causal, ws_ref[h], 0.0).astype(_BF16)
        lanes = slice(h * LANES, (h + 1) * LANES)
        for c in range(tm // CHUNK):
            rows = slice(c * CHUNK, (c + 1) * CHUNK)
            s = _dot(w_h, v_ref[rows, lanes]) + bs_ref[:, lanes]
            gated_ref[rows, lanes] = (z_ref[rows, lanes] * s).astype(_BF16)
    o_ref[...] = x + _dot(gated_ref[...], w_out_ref[...])


def _mixer_a(x, g, w_in, gv, ws, bs, w_out):
    m = x.shape[0]
    tm = TM_A
    resident = [_nbytes(w_in.shape, _BF16), _nbytes(w_out.shape, _BF16), _nbytes(ws.shape, _F32),
                _nbytes(bs.shape, _F32)]
    streamed = [2 * _nbytes((tm, D_MODEL), _F32)]
    scratch = [_nbytes((tm, 2 * D_MODEL), _F32), 2 * _nbytes((tm, D_MODEL), _BF16)]
    temporaries = [2 * _nbytes((tm, 2 * D_MODEL), _F32)]
    return pl.pallas_call(
        _mixer_a_kernel,
        out_shape=jax.ShapeDtypeStruct((m, D_MODEL), _F32),
        grid=(m // tm,),
        in_specs=[
            _row_spec(tm, D_MODEL),
            _const_spec(g.shape),
            _const_spec(w_in.shape),
            _const_spec(gv.shape),
            _const_spec(ws.shape),
            _const_spec(bs.shape),
            _const_spec(w_out.shape),
        ],
        out_specs=_row_spec(tm, D_MODEL),
        scratch_shapes=[
            pltpu.VMEM((tm, 2 * D_MODEL), _F32),
            pltpu.VMEM((tm, D_MODEL), _BF16),
            pltpu.VMEM((tm, D_MODEL), _BF16),
        ],
        compiler_params=pltpu.CompilerParams(
            dimension_semantics=("arbitrary",),
            vmem_limit_bytes=_vmem_limit(resident, streamed, scratch, temporaries)),
        name="mixer_a",
    )(x, g, w_in, gv, ws, bs, w_out)


def _ffn_ple_kernel(final_norm, h_ref, p_ref, gf_ref, wup_ref, cw_ref, cb_ref, wdown_ref,
                    gp_ref, wpi_ref, wg_ref, bg_ref, ge_ref, *rest):
    if final_norm:
        o_ref, slab_ref, hn_ref, carry_ref, act_ref = rest
    else:
        we_ref, o_ref, kv_ref, slab_ref, hn_ref, carry_ref, act_ref = rest
    tm = h_ref.shape[0]
    first_of_seq = (pl.program_id(0) % (SEQ // tm)) == 0
    n_groups = tm // GROUP
    n_slabs = D_MODEL // LANES
    chunk_tiles = FF_CHUNK // LANES
    lanes = lambda l: slice(l * LANES, (l + 1) * LANES)

    h = h_ref[...]
    hn = _rms(h, gf_ref[...])
    for j in range(n_slabs):
        slab_ref[j] = hn[:, lanes(j)]
    for j in range(n_slabs):
        for k in range(n_groups):
            for q in range(0, GROUP_TILES, 2):
                pair = [slab_ref[j, pl.ds(GROUP * k + q + d, SUBLANES, stride=GROUP_TILES), :] for d in range(2)]
                rows = slice(GROUP * k + SUBLANES * q, GROUP * k + SUBLANES * (q + 2))
                hn_ref[rows, lanes(j)] = jnp.concatenate(pair, axis=0).astype(_BF16)

    first_row = lax.broadcasted_iota(jnp.int32, (SUBLANES, LANES), 0) == 0
    tails = {}

    def conv_block(c, half, j, k):
        col = slab_ref.at[2 * chunk_tiles * (c % 2) + chunk_tiles * half + j]
        blk = col[GROUP * k:GROUP * (k + 1), :]
        prev = tails[c, half][:, lanes(j)] if k == 0 else col[GROUP * k - CARRY_ROWS:GROUP * k, :]
        start = half * D_FF + c * FF_CHUNK + j * LANES
        wl = slice(start, start + LANES)
        edge2, edge1 = (
            jnp.where(first_row, pltpu.roll(prev[s:s + SUBLANES], 1, 0),
                      pltpu.roll(blk[GROUP - 2 * SUBLANES + s:GROUP - SUBLANES + s], 1, 0))
            for s in (0, SUBLANES))
        back1 = jnp.concatenate([edge1, blk[:GROUP - SUBLANES]], axis=0)
        back2 = jnp.concatenate([edge2, edge1, blk[:GROUP - 2 * SUBLANES]], axis=0)
        w0, w1, w2 = (cw_ref[t:t + 1, wl] for t in range(CONV_WIDTH))
        return cb_ref[:, wl] + back2 * w0 + back1 * w1 + blk * w2

    for c in range(N_FF_CHUNKS + 1):
        if c < N_FF_CHUNKS:
            for half in range(2):
                start = half * D_FF + c * FF_CHUNK
                up = _dot(hn_ref[...], wup_ref[:, start:start + FF_CHUNK])
                tails[c, half] = jnp.where(first_of_seq, 0.0, carry_ref[half, c])
                carry_ref[half, c] = up[tm - CARRY_ROWS:, :]
                for j in range(chunk_tiles):
                    slab_ref[2 * chunk_tiles * (c % 2) + chunk_tiles * half + j] = up[:, lanes(j)]
        if c >= 1:
            for j in range(chunk_tiles):
                for k in range(n_groups):
                    act_ref[GROUP * k:GROUP * (k + 1), lanes((c - 1) * chunk_tiles + j)] = (
                        jax.nn.silu(conv_block(c - 1, 0, j, k)) * conv_block(c - 1, 1, j, k)).astype(_BF16)
        if c == DOWN_SPLIT_CHUNKS:
            down_head = _dot(act_ref[:, :DOWN_SPLIT_CHUNKS * FF_CHUNK], wdown_ref[:DOWN_SPLIT_CHUNKS * FF_CHUNK, :])

    down = down_head + _dot(act_ref[:, DOWN_SPLIT_CHUNKS * FF_CHUNK:], wdown_ref[DOWN_SPLIT_CHUNKS * FF_CHUNK:, :])
    for j in range(n_slabs):
        for k in range(n_groups):
            for q in range(GROUP_TILES):
                rows = slice(GROUP * k + SUBLANES * q, GROUP * k + SUBLANES * (q + 1))
                slab_ref[j, pl.ds(GROUP * k + q, SUBLANES, stride=GROUP_TILES), :] = down[rows, lanes(j)]
    h = h + jnp.concatenate([slab_ref[j] for j in range(n_slabs)], axis=1)

    hn = _rms(h, gp_ref[...]).astype(_BF16)
    gate = jax.nn.sigmoid(_dot(hn, wg_ref[...]) + bg_ref[...])
    h = h + _dot(p_ref[...].astype(_BF16), wpi_ref[...]) * gate

    if final_norm:
        o_ref[...] = _rms(h, ge_ref[...])
    else:
        o_ref[...] = h
        kv_ref[...] = _dot(_rms(h, ge_ref[...]).astype(_BF16), we_ref[...]).astype(_BF16)


def _layer_spec(shape, layer):
    zeros = (0,) * (len(shape) - 1)
    return pl.BlockSpec((None,) + tuple(shape[1:]), lambda i: (layer,) + zeros, pipeline_mode=pl.Buffered(1))


def _ffn_ple(h, p, layer, stacked, ge, we=None, *, name):
    m = h.shape[0]
    tm = TM_F
    final_norm = we is None
    consts = (ge,) + (() if final_norm else (we,))
    out_shape = [jax.ShapeDtypeStruct((m, D_MODEL), _F32)]
    out_specs = [_row_spec(tm, D_MODEL)]
    streamed = [2 * _nbytes((tm, D_MODEL), _F32), _nbytes((tm, PLE_DIM), _F32)]
    if not final_norm:
        out_shape.append(jax.ShapeDtypeStruct((m, we.shape[1]), _BF16))
        out_specs.append(_row_spec(tm, we.shape[1]))
        streamed.append(_nbytes((tm, we.shape[1]), _BF16))
    scratch_shapes = [
        pltpu.VMEM((D_MODEL // LANES, tm, LANES), _F32),
        pltpu.VMEM((tm, D_MODEL), _BF16),
        pltpu.VMEM((2, N_FF_CHUNKS, CARRY_ROWS, FF_CHUNK), _F32),
        pltpu.VMEM((tm, D_FF), _BF16),
    ]
    resident = [_nbytes(w.shape[1:], w.dtype) for w in stacked] + [_nbytes(w.shape, w.dtype) for w in consts]
    scratch = [_nbytes(s.shape, s.dtype) for s in scratch_shapes]
    temporaries = [4 * _nbytes((tm, D_MODEL), _F32), 2 * _nbytes((tm, 2 * FF_CHUNK), _F32)]
    outs = pl.pallas_call(
        functools.partial(_ffn_ple_kernel, final_norm),
        out_shape=out_shape,
        grid=(m // tm,),
        in_specs=([_row_spec(tm, D_MODEL), pl.BlockSpec((None, tm, PLE_DIM), lambda i: (layer, i, 0))]
                  + [_layer_spec(w.shape, layer) for w in stacked] + [_const_spec(w.shape) for w in consts]),
        out_specs=out_specs,
        scratch_shapes=scratch_shapes,
        compiler_params=pltpu.CompilerParams(
            dimension_semantics=("arbitrary",),
            vmem_limit_bytes=_vmem_limit(resident, streamed, scratch, temporaries)),
        name=name,
    )(h, p, *stacked, *consts)
    return outs[0] if final_norm else outs


def _pair_heads(w, axis):
    shape = w.shape
    split = shape[:axis] + (N_KV_HEADS // 2, 2, GQA_GROUP, HEAD_DIM) + shape[axis + 1:]
    return jnp.swapaxes(w.reshape(split), axis + 1, axis + 2).reshape(shape)


def _attention_bias():
    i = np.arange(CHUNK)[:, None]
    j = np.arange(2 * CHUNK)[None, :]
    dist = (i + CHUNK - j).astype(np.float32)
    in_band = (dist >= 0) & (dist < CHUNK)
    slopes = np.array([2.0 ** (-8.0 * (h + 1) / N_Q_HEADS) for h in range(N_Q_HEADS)], dtype=np.float32)
    bias = -slopes[:, None, None] * dist[None]
    general = np.where(in_band[None], bias, -np.inf)
    first = np.where((in_band & (j >= CHUNK))[None], bias, -np.inf)
    return np.stack([general, first]).astype(np.float32)


def _mixer_b_kernel(sinks_ref, h_ref, kvc_ref, kvp_ref, g_ref, wq_ref, wo_ref, bias_ref, o_ref,
                    q_ref, att_ref):
    tm = h_ref.shape[0]
    blocks_per_seq = SEQ // CHUNK
    blocks_per_tile = tm // CHUNK
    kv_width = N_KV_HEADS * HEAD_DIM

    lane = lax.broadcasted_iota(jnp.int32, (2 * CHUNK, LANES), 1)
    low_half = lane < HEAD_DIM

    def project_q(part):
        rows = slice(part * SUB_B, (part + 1) * SUB_B)
        q = _dot(_rms(h_ref[rows, :], g_ref[...]).astype(_BF16), wq_ref[...]) * (HEAD_DIM ** -0.5)
        q_ref[rows, :] = q.astype(_BF16)

    def project_o(part):
        rows = slice(part * SUB_B, (part + 1) * SUB_B)
        o_ref[rows, :] = h_ref[rows, :] + _dot(att_ref[rows, :], wo_ref[...])

    project_q(0)
    for b in range(blocks_per_tile):
        if b % (SUB_B // CHUNK) == 0 and (b + SUB_B // CHUNK) < blocks_per_tile:
            project_q(b // (SUB_B // CHUNK) + 1)
        rows = slice(b * CHUNK, (b + 1) * CHUNK)
        block = pl.program_id(0) * blocks_per_tile + b
        first = jnp.where(block % blocks_per_seq == 0, 1, 0)
        if b == 0:
            kv_prev = kvp_ref[...]
        else:
            kv_prev = kvc_ref[(b - 1) * CHUNK:b * CHUNK, :]
        kv = jnp.concatenate([kv_prev, kvc_ref[rows, :]], axis=0)
        for pair in range(N_KV_HEADS // 2):
            k_pair = kv[:, pair * LANES:(pair + 1) * LANES]
            v_pair = kv[:, kv_width + pair * LANES:kv_width + (pair + 1) * LANES]
            groups = [pair * GQA_GROUP + g for g in range(GQA_GROUP)]
            q_stack = jnp.concatenate([q_ref[rows, j * LANES:(j + 1) * LANES] for j in groups], axis=0)
            probs = []
            for half in range(2):
                kv_head = 2 * pair + half
                k_half = jnp.where(low_half if half == 0 else ~low_half, k_pair, 0)
                s = lax.dot_general(q_stack, k_half, (((1,), (1,)), ((), ())),
                                    preferred_element_type=_F32)
                parts = []
                for g in range(GQA_GROUP):
                    head = kv_head * GQA_GROUP + g
                    sg = s[g * CHUNK:(g + 1) * CHUNK, :] + bias_ref[first, head]
                    sink = sinks_ref[head]
                    mx = jnp.maximum(jnp.max(sg, axis=-1, keepdims=True), sink)
                    e = jnp.exp(sg - mx)
                    denom = jnp.sum(e, axis=-1, keepdims=True) + jnp.exp(sink - mx)
                    parts.append((e * (1.0 / denom)).astype(_BF16))
                probs.append(jnp.concatenate(parts, axis=0))
            p_cat = jnp.concatenate(probs, axis=1)
            v_cat = jnp.concatenate([jnp.where(low_half, v_pair, 0), jnp.where(low_half, 0, v_pair)], axis=0)
            out = _dot(p_cat, v_cat)
            for g in range(GQA_GROUP):
                j = groups[g]
                att_ref[rows, j * LANES:(j + 1) * LANES] = out[g * CHUNK:(g + 1) * CHUNK, :].astype(_BF16)
        if (b + 1) % (SUB_B // CHUNK) == 0:
            project_o(b // (SUB_B // CHUNK))


def _mixer_b(h, kv, g, wq, wo, sinks, bias):
    m = h.shape[0]
    tm = TM_B
    blocks_per_tile = tm // CHUNK
    kvw = kv.shape[1]
    resident = [_nbytes(wq.shape, _BF16), _nbytes(wo.shape, _BF16), _nbytes(bias.shape, _F32)]
    streamed = [2 * _nbytes((tm, D_MODEL), _F32), _nbytes((tm + CHUNK, kvw), _BF16)]
    scratch = [2 * _nbytes((tm, D_MODEL), _BF16)]
    temporaries = [3 * _nbytes((tm, D_MODEL), _F32), 8 * _nbytes((GQA_GROUP * CHUNK, 2 * CHUNK), _F32)]

    return pl.pallas_call(
        _mixer_b_kernel,
        out_shape=jax.ShapeDtypeStruct((m, D_MODEL), _F32),
        grid=(m // tm,),
        in_specs=[
            pl.BlockSpec(memory_space=pltpu.SMEM),
            _row_spec(tm, D_MODEL),
            _row_spec(tm, kvw),
            pl.BlockSpec((CHUNK, kvw), lambda i: (jnp.maximum(i * blocks_per_tile - 1, 0), 0)),
            _const_spec(g.shape),
            _const_spec(wq.shape),
            _const_spec(wo.shape),
            _const_spec(bias.shape),
        ],
        out_specs=_row_spec(tm, D_MODEL),
        scratch_shapes=[pltpu.VMEM((tm, D_MODEL), _BF16), pltpu.VMEM((tm, D_MODEL), _BF16)],
        compiler_params=pltpu.CompilerParams(
            dimension_semantics=("arbitrary",),
            vmem_limit_bytes=_vmem_limit(resident, streamed, scratch, temporaries)),
        name="mixer_b",
    )(sinks, h, kv, kv, g, wq, wo, bias)


def kernel(x, p, norm_mix, norm_ffn, norm_ple, norm_kv, norm_final, a_w_in, a_norm_v, a_w_s, a_b_s, a_w_out, w_kv, b_w_q, b_sinks, b_w_o, f_w_up, f_conv_w, f_conv_b, f_w_down, ple_w_in, ple_w_gate, ple_b_gate):
    batch, seq, d = x.shape
    assert (seq, d) == (SEQ, D_MODEL) and seq % max(TM_A, TM_F, TM_B) == 0
    m = batch * seq
    row = lambda a: a.reshape(1, -1)

    h = x.reshape(m, d)
    bias_s = jnp.repeat(a_b_s[0].T, LANES, axis=1)
    h = _mixer_a(h, row(norm_mix[0]), a_w_in[0].astype(_BF16), row(a_norm_v[0]), a_w_s[0], bias_s,
                 a_w_out[0].astype(_BF16))

    rows = lambda a: a[:, None, :]
    stacked = (rows(norm_ffn), f_w_up.astype(_BF16), f_conv_w, rows(f_conv_b), f_w_down.astype(_BF16),
               rows(norm_ple), ple_w_in.astype(_BF16), ple_w_gate.astype(_BF16), rows(ple_b_gate))
    p = p.reshape(p.shape[0], m, PLE_DIM)

    h, kv = _ffn_ple(h, p, 0, stacked, row(norm_kv), w_kv.astype(_BF16), name="ffn_ple_0")

    h = _mixer_b(h, kv, row(norm_mix[1]), _pair_heads(b_w_q[0], 1).astype(_BF16),
                 _pair_heads(b_w_o[0], 0).astype(_BF16), b_sinks[0], jnp.asarray(_attention_bias()))

    out = _ffn_ple(h, p, 1, stacked, row(norm_final), name="ffn_ple_1")
    return out.reshape(batch, seq, d)
```

```python
import functools
import math

import jax
import jax.numpy as jnp
import numpy as np
from jax import lax
from jax.experimental import pallas as pl
from jax.experimental.pallas import tpu as pltpu

D_MODEL = 1024
SEQ = 8192
CHUNK = 128
A_GROUPS = 8
HEAD_DIM = 64
N_Q_HEADS = 16
N_KV_HEADS = 4
GQA_GROUP = N_Q_HEADS // N_KV_HEADS
D_FF = 2816
CONV_WIDTH = 3
PLE_DIM = 256
EPS = 1e-6
LOG2E = math.log2(math.e)

LANES = 128
SUBLANES = 8
MXU_WIDTH = 256
V7X_VMEM_BYTES = 64 * 1024 * 1024

TM_A = 1024
TM_F = 1024
TM_B = 1024
FF_CHUNK = MXU_WIDTH
N_FF_CHUNKS = D_FF // FF_CHUNK
GROUP_TILES = 4
GROUP = SUBLANES * GROUP_TILES
CARRY_ROWS = (CONV_WIDTH - 1) * SUBLANES

_BF16 = jnp.bfloat16
_F32 = jnp.float32


def _dot(a, b):
    return jnp.dot(a, b, preferred_element_type=_F32)


def _rms(x, g):
    return x * lax.rsqrt(jnp.mean(x * x, axis=-1, keepdims=True) + EPS) * g


def _const_spec(shape):
    zeros = (0,) * len(shape)
    return pl.BlockSpec(shape, lambda i: zeros, pipeline_mode=pl.Buffered(1))


def _row_spec(tm, width):
    return pl.BlockSpec((tm, width), lambda i: (i, 0))


def _nbytes(shape, dtype):
    return math.prod(shape) * jnp.dtype(dtype).itemsize


def _vmem_limit(resident, streamed, scratch, temporaries):
    total = sum(resident) + 2 * sum(streamed) + sum(scratch) + sum(temporaries)
    return min(int(total), V7X_VMEM_BYTES)


def _mixer_a_kernel(x_ref, g_ref, w_in_ref, gv_ref, ws_ref, bs_ref, w_out_ref, o_ref,
                    z_ref, v_ref, gated_ref):
    tm = x_ref.shape[0]
    x = x_ref[...]
    xn = _rms(x, g_ref[...]).astype(_BF16)
    z = _dot(xn, w_in_ref[...])
    z_ref[...] = jax.nn.gelu(z, approximate=True)
    v_ref[...] = _rms(z_ref[:, D_MODEL:], gv_ref[...]).astype(_BF16)

    row = lax.broadcasted_iota(jnp.int32, (CHUNK, CHUNK), 0)
    col = lax.broadcasted_iota(jnp.int32, (CHUNK, CHUNK), 1)
    causal = row >= col
    for h in range(A_GROUPS):
        w_h = jnp.where(causal, ws_ref[h], 0.0).astype(_BF16)
        lanes = slice(h * LANES, (h + 1) * LANES)
        for c in range(tm // CHUNK):
            rows = slice(c * CHUNK, (c + 1) * CHUNK)
            s = _dot(w_h, v_ref[rows, lanes]) + bs_ref[:, lanes]
            gated_ref[rows, lanes] = (z_ref[rows, lanes] * s).astype(_BF16)
    o_ref[...] = x + _dot(gated_ref[...], w_out_ref[...])


def _mixer_a(x, g, w_in, gv, ws, bs, w_out):
    m = x.shape[0]
    tm = TM_A
    resident = [_nbytes(w_in.shape, _BF16), _nbytes(w_out.shape, _BF16), _nbytes(ws.shape, _F32),
                _nbytes(bs.shape, _F32)]
    streamed = [2 * _nbytes((tm, D_MODEL), _F32)]
    scratch = [_nbytes((tm, 2 * D_MODEL), _F32), 2 * _nbytes((tm, D_MODEL), _BF16)]
    temporaries = [2 * _nbytes((tm, 2 * D_MODEL), _F32)]
    return pl.pallas_call(
        _mixer_a_kernel,
        out_shape=jax.ShapeDtypeStruct((m, D_MODEL), _F32),
        grid=(m // tm,),
        in_specs=[
            _row_spec(tm, D_MODEL),
            _const_spec(g.shape),
            _const_spec(w_in.shape),
            _const_spec(gv.shape),
            _const_spec(ws.shape),
            _const_spec(bs.shape),
            _const_spec(w_out.shape),
        ],
        out_specs=_row_spec(tm, D_MODEL),
        scratch_shapes=[
            pltpu.VMEM((tm, 2 * D_MODEL), _F32),
            pltpu.VMEM((tm, D_MODEL), _BF16),
            pltpu.VMEM((tm, D_MODEL), _BF16),
        ],
        compiler_params=pltpu.CompilerParams(
            dimension_semantics=("arbitrary",),
            vmem_limit_bytes=_vmem_limit(resident, streamed, scratch, temporaries)),
        name="mixer_a",
    )(x, g, w_in, gv, ws, bs, w_out)


def _ffn_ple_kernel(final_norm, h_ref, p_ref, gf_ref, wup_ref, cw_ref, cb_ref, wdown_ref,
                    gp_ref, wpi_ref, wg_ref, bg_ref, ge_ref, *rest):
    if final_norm:
        o_ref, slab_ref, hn_ref, carry_ref, act_ref = rest
    else:
        we_ref, o_ref, kv_ref, slab_ref, hn_ref, carry_ref, act_ref = rest
    tm = h_ref.shape[0]
    first_of_seq = (pl.program_id(0) % (SEQ // tm)) == 0
    n_groups = tm // GROUP
    n_slabs = D_MODEL // LANES
    chunk_tiles = FF_CHUNK // LANES
    lanes = lambda l: slice(l * LANES, (l + 1) * LANES)

    h = h_ref[...]
    hn = _rms(h, gf_ref[...])
    for j in range(n_slabs):
        slab_ref[j] = hn[:, lanes(j)]
    for j in range(n_slabs):
        for k in range(n_groups):
            for q in range(0, GROUP_TILES, 2):
                pair = [slab_ref[j, pl.ds(GROUP * k + q + d, SUBLANES, stride=GROUP_TILES), :] for d in range(2)]
                rows = slice(GROUP * k + SUBLANES * q, GROUP * k + SUBLANES * (q + 2))
                hn_ref[rows, lanes(j)] = jnp.concatenate(pair, axis=0).astype(_BF16)

    first_row = lax.broadcasted_iota(jnp.int32, (SUBLANES, LANES), 0) == 0
    tails = {}

    def conv_block(c, half, j, k):
        col = slab_ref.at[2 * chunk_tiles * (c % 2) + chunk_tiles * half + j]
        blk = col[GROUP * k:GROUP * (k + 1), :]
        prev = tails[c, half][:, lanes(j)] if k == 0 else col[GROUP * k - CARRY_ROWS:GROUP * k, :]
        start = half * D_FF + c * FF_CHUNK + j * LANES
        wl = slice(start, start + LANES)
        edge2, edge1 = (
            jnp.where(first_row, pltpu.roll(prev[s:s + SUBLANES], 1, 0),
                      pltpu.roll(blk[GROUP - 2 * SUBLANES + s:GROUP - SUBLANES + s], 1, 0))
            for s in (0, SUBLANES))
        back1 = jnp.concatenate([edge1, blk[:GROUP - SUBLANES]], axis=0)
        back2 = jnp.concatenate([edge2, edge1, blk[:GROUP - 2 * SUBLANES]], axis=0)
        w0, w1, w2 = (cw_ref[t:t + 1, wl] for t in range(CONV_WIDTH))
        return cb_ref[:, wl] + back2 * w0 + back1 * w1 + blk * w2

    for c in range(N_FF_CHUNKS + 1):
        if c < N_FF_CHUNKS:
            for half in range(2):
                start = half * D_FF + c * FF_CHUNK
                up = _dot(hn_ref[...], wup_ref[:, start:start + FF_CHUNK])
                tails[c, half] = jnp.where(first_of_seq, 0.0, carry_ref[half, c])
                carry_ref[half, c] = up[tm - CARRY_ROWS:, :]
                for j in range(chunk_tiles):
                    slab_ref[2 * chunk_tiles * (c % 2) + chunk_tiles * half + j] = up[:, lanes(j)]
        if c >= 1:
            for j in range(chunk_tiles):
                for k in range(n_groups):
                    act_ref[GROUP * k:GROUP * (k + 1), lanes((c - 1) * chunk_tiles + j)] = (
                        jax.nn.silu(conv_block(c - 1, 0, j, k)) * conv_block(c - 1, 1, j, k)).astype(_BF16)

    down = _dot(act_ref[...], wdown_ref[...])
    for j in range(n_slabs):
        for k in range(n_groups):
            for q in range(GROUP_TILES):
                rows = slice(GROUP * k + SUBLANES * q, GROUP * k + SUBLANES * (q + 1))
                slab_ref[j, pl.ds(GROUP * k + q, SUBLANES, stride=GROUP_TILES), :] = down[rows, lanes(j)]
    h = h + jnp.concatenate([slab_ref[j] for j in range(n_slabs)], axis=1)

    hn = _rms(h, gp_ref[...]).astype(_BF16)
    gate = jax.nn.sigmoid(_dot(hn, wg_ref[...]) + bg_ref[...])
    h = h + _dot(p_ref[...].astype(_BF16), wpi_ref[...]) * gate

    if final_norm:
        o_ref[...] = _rms(h, ge_ref[...])
    else:
        o_ref[...] = h
        kv_ref[...] = _dot(_rms(h, ge_ref[...]).astype(_BF16), we_ref[...]).astype(_BF16)


def _layer_spec(shape, layer):
    zeros = (0,) * (len(shape) - 1)
    return pl.BlockSpec((None,) + tuple(shape[1:]), lambda i: (layer,) + zeros, pipeline_mode=pl.Buffered(1))


def _ffn_ple(h, p, layer, stacked, ge, we=None, *, name):
    m = h.shape[0]
    tm = TM_F
    final_norm = we is None
    consts = (ge,) + (() if final_norm else (we,))
    out_shape = [jax.ShapeDtypeStruct((m, D_MODEL), _F32)]
    out_specs = [_row_spec(tm, D_MODEL)]
    streamed = [2 * _nbytes((tm, D_MODEL), _F32), _nbytes((tm, PLE_DIM), _F32)]
    if not final_norm:
        out_shape.append(jax.ShapeDtypeStruct((m, we.shape[1]), _BF16))
        out_specs.append(_row_spec(tm, we.shape[1]))
        streamed.append(_nbytes((tm, we.shape[1]), _BF16))
    scratch_shapes = [
        pltpu.VMEM((D_MODEL // LANES, tm, LANES), _F32),
        pltpu.VMEM((tm, D_MODEL), _BF16),
        pltpu.VMEM((2, N_FF_CHUNKS, CARRY_ROWS, FF_CHUNK), _F32),
        pltpu.VMEM((tm, D_FF), _BF16),
    ]
    resident = [_nbytes(w.shape[1:], w.dtype) for w in stacked] + [_nbytes(w.shape, w.dtype) for w in consts]
    scratch = [_nbytes(s.shape, s.dtype) for s in scratch_shapes]
    temporaries = [4 * _nbytes((tm, D_MODEL), _F32), 2 * _nbytes((tm, 2 * FF_CHUNK), _F32)]
    outs = pl.pallas_call(
        functools.partial(_ffn_ple_kernel, final_norm),
        out_shape=out_shape,
        grid=(m // tm,),
        in_specs=([_row_spec(tm, D_MODEL), pl.BlockSpec((None, tm, PLE_DIM), lambda i: (layer, i, 0))]
                  + [_layer_spec(w.shape, layer) for w in stacked] + [_const_spec(w.shape) for w in consts]),
        out_specs=out_specs,
        scratch_shapes=scratch_shapes,
        compiler_params=pltpu.CompilerParams(
            dimension_semantics=("arbitrary",),
            vmem_limit_bytes=_vmem_limit(resident, streamed, scratch, temporaries)),
        name=name,
    )(h, p, *stacked, *consts)
    return outs[0] if final_norm else outs


def _pair_heads(w, axis):
    shape = w.shape
    split = shape[:axis] + (N_KV_HEADS // 2, 2, GQA_GROUP, HEAD_DIM) + shape[axis + 1:]
    return jnp.swapaxes(w.reshape(split), axis + 1, axis + 2).reshape(shape)


def _attention_bias():
    i = np.arange(CHUNK)[:, None]
    j = np.arange(2 * CHUNK)[None, :]
    dist = (i + CHUNK - j).astype(np.float32)
    in_band = (dist >= 0) & (dist < CHUNK)
    slopes = np.array([2.0 ** (-8.0 * (h + 1) / N_Q_HEADS) for h in range(N_Q_HEADS)], dtype=np.float32)
    bias = -slopes[:, None, None] * dist[None] * LOG2E
    general = np.where(in_band[None], bias, -np.inf)
    first = np.where((in_band & (j >= CHUNK))[None], bias, -np.inf)
    return np.stack([general, first]).astype(np.float32)


def _mixer_b_kernel(sinks_ref, h_ref, kvc_ref, kvp_ref, g_ref, wq_ref, wo_ref, bias_ref, o_ref,
                    q_ref, att_ref):
    tm = h_ref.shape[0]
    blocks_per_seq = SEQ // CHUNK
    blocks_per_tile = tm // CHUNK
    kv_width = N_KV_HEADS * HEAD_DIM

    h = h_ref[...]
    q = _dot(_rms(h, g_ref[...]).astype(_BF16), wq_ref[...]) * (HEAD_DIM ** -0.5 * LOG2E)
    q_ref[...] = q.astype(_BF16)

    low_half = lax.broadcasted_iota(jnp.int32, (2 * CHUNK, LANES), 1) < HEAD_DIM
    low_half_q = lax.broadcasted_iota(jnp.int32, (GQA_GROUP * CHUNK, LANES), 1) < HEAD_DIM
    for b in range(blocks_per_tile):
        rows = slice(b * CHUNK, (b + 1) * CHUNK)
        block = pl.program_id(0) * blocks_per_tile + b
        first = jnp.where(block % blocks_per_seq == 0, 1, 0)
        if b == 0:
            kv_prev = kvp_ref[...]
        else:
            kv_prev = kvc_ref[(b - 1) * CHUNK:b * CHUNK, :]
        kv = jnp.concatenate([kv_prev, kvc_ref[rows, :]], axis=0)
        for pair in range(N_KV_HEADS // 2):
            k_pair = kv[:, pair * LANES:(pair + 1) * LANES]
            v_pair = kv[:, kv_width + pair * LANES:kv_width + (pair + 1) * LANES]
            groups = [pair * GQA_GROUP + g for g in range(GQA_GROUP)]
            q_stack = jnp.concatenate([q_ref[rows, j * LANES:(j + 1) * LANES] for j in groups], axis=0)
            weights, recips = [], []
            for half in range(2):
                kv_head = 2 * pair + half
                k_half = jnp.where(low_half if half == 0 else ~low_half, k_pair, 0)
                s = lax.dot_general(q_stack, k_half, (((1,), (1,)), ((), ())),
                                    preferred_element_type=_F32)
                parts, rparts = [], []
                for g in range(GQA_GROUP):
                    head = kv_head * GQA_GROUP + g
                    sg = s[g * CHUNK:(g + 1) * CHUNK, :] + bias_ref[first, head]
                    sink = sinks_ref[head] * LOG2E
                    mx = jnp.maximum(jnp.max(sg, axis=-1, keepdims=True), sink)
                    e = jnp.exp2(sg - mx)
                    denom = jnp.sum(e, axis=-1, keepdims=True) + jnp.exp2(sink - mx)
                    parts.append(e.astype(_BF16))
                    rparts.append(1.0 / denom)
                weights.append(jnp.concatenate(parts, axis=0))
                recips.append(jnp.concatenate(rparts, axis=0))
            p_cat = jnp.concatenate(weights, axis=1)
            v_cat = jnp.concatenate([jnp.where(low_half, v_pair, 0), jnp.where(low_half, 0, v_pair)], axis=0)
            out = _dot(p_cat, v_cat) * jnp.where(low_half_q, recips[0], recips[1])
            for g in range(GQA_GROUP):
                j = groups[g]
                att_ref[rows, j * LANES:(j + 1) * LANES] = out[g * CHUNK:(g + 1) * CHUNK, :].astype(_BF16)
    o_ref[...] = h + _dot(att_ref[...], wo_ref[...])


def _mixer_b(h, kv, g, wq, wo, sinks, bias):
    m = h.shape[0]
    tm = TM_B
    blocks_per_tile = tm // CHUNK
    kvw = kv.shape[1]
    resident = [_nbytes(wq.shape, _BF16), _nbytes(wo.shape, _BF16), _nbytes(bias.shape, _F32)]
    streamed = [2 * _nbytes((tm, D_MODEL), _F32), _nbytes((tm + CHUNK, kvw), _BF16)]
    scratch = [2 * _nbytes((tm, D_MODEL), _BF16)]
    temporaries = [3 * _nbytes((tm, D_MODEL), _F32), 8 * _nbytes((GQA_GROUP * CHUNK, 2 * CHUNK), _F32)]

    return pl.pallas_call(
        _mixer_b_kernel,
        out_shape=jax.ShapeDtypeStruct((m, D_MODEL), _F32),
        grid=(m // tm,),
        in_specs=[
            pl.BlockSpec(memory_space=pltpu.SMEM),
            _row_spec(tm, D_MODEL),
            _row_spec(tm, kvw),
            pl.BlockSpec((CHUNK, kvw), lambda i: (jnp.maximum(i * blocks_per_tile - 1, 0), 0)),
            _const_spec(g.shape),
            _const_spec(wq.shape),
            _const_spec(wo.shape),
            _const_spec(bias.shape),
        ],
        out_specs=_row_spec(tm, D_MODEL),
        scratch_shapes=[pltpu.VMEM((tm, D_MODEL), _BF16), pltpu.VMEM((tm, D_MODEL), _BF16)],
        compiler_params=pltpu.CompilerParams(
            dimension_semantics=("arbitrary",),
            vmem_limit_bytes=_vmem_limit(resident, streamed, scratch, temporaries)),
        name="mixer_b",
    )(sinks, h, kv, kv, g, wq, wo, bias)


def kernel(x, p, norm_mix, norm_ffn, norm_ple, norm_kv, norm_final, a_w_in, a_norm_v, a_w_s, a_b_s, a_w_out, w_kv, b_w_q, b_sinks, b_w_o, f_w_up, f_conv_w, f_conv_b, f_w_down, ple_w_in, ple_w_gate, ple_b_gate):
    batch, seq, d = x.shape
    assert (seq, d) == (SEQ, D_MODEL) and seq % max(TM_A, TM_F, TM_B) == 0
    m = batch * seq
    row = lambda a: a.reshape(1, -1)

    h = x.reshape(m, d)
    bias_s = jnp.repeat(a_b_s[0].T, LANES, axis=1)
    h = _mixer_a(h, row(norm_mix[0]), a_w_in[0].astype(_BF16), row(a_norm_v[0]), a_w_s[0], bias_s,
                 a_w_out[0].astype(_BF16))

    rows = lambda a: a[:, None, :]
    stacked = (rows(norm_ffn), f_w_up.astype(_BF16), f_conv_w, rows(f_conv_b), f_w_down.astype(_BF16),
               rows(norm_ple), ple_w_in.astype(_BF16), ple_w_gate.astype(_BF16), rows(ple_b_gate))
    p = p.reshape(p.shape[0], m, PLE_DIM)

    h, kv = _ffn_ple(h, p, 0, stacked, row(norm_kv), w_kv.astype(_BF16), name="ffn_ple_0")

    h = _mixer_b(h, kv, row(norm_mix[1]), _pair_heads(b_w_q[0], 1).astype(_BF16),
                 _pair_heads(b_w_o[0], 0).astype(_BF16), b_sinks[0], jnp.asarray(_attention_bias()))

    out = _ffn_ple(h, p, 1, stacked, row(norm_final), name="ffn_ple_1")
    return out.reshape(batch, seq, d)
```

```python
import functools
import math

import jax
import jax.numpy as jnp
import numpy as np
from jax import lax
from jax.experimental import pallas as pl
from jax.experimental.pallas import tpu as pltpu

D_MODEL = 1024
SEQ = 8192
CHUNK = 128
A_GROUPS = 8
HEAD_DIM = 64
N_Q_HEADS = 16
N_KV_HEADS = 4
GQA_GROUP = N_Q_HEADS // N_KV_HEADS
D_FF = 2816
CONV_WIDTH = 3
PLE_DIM = 256
EPS = 1e-6
LOG2E = math.log2(math.e)

LANES = 128
SUBLANES = 8
MXU_WIDTH = 256
V7X_VMEM_BYTES = 64 * 1024 * 1024

TM_A = 1024
TM_F = 1024
TM_B = 1024
FF_CHUNK = MXU_WIDTH
N_FF_CHUNKS = D_FF // FF_CHUNK
GROUP_TILES = 4
GROUP = SUBLANES * GROUP_TILES
CARRY_ROWS = (CONV_WIDTH - 1) * SUBLANES

_BF16 = jnp.bfloat16
_F32 = jnp.float32


def _dot(a, b):
    return jnp.dot(a, b, preferred_element_type=_F32)


def _rms(x, g):
    return x * lax.rsqrt(jnp.mean(x * x, axis=-1, keepdims=True) + EPS) * g


def _const_spec(shape):
    zeros = (0,) * len(shape)
    return pl.BlockSpec(shape, lambda i: zeros, pipeline_mode=pl.Buffered(1))


def _row_spec(tm, width):
    return pl.BlockSpec((tm, width), lambda i: (i, 0))


def _nbytes(shape, dtype):
    return math.prod(shape) * jnp.dtype(dtype).itemsize


def _vmem_limit(resident, streamed, scratch, temporaries):
    total = sum(resident) + 2 * sum(streamed) + sum(scratch) + sum(temporaries)
    return min(int(total), V7X_VMEM_BYTES)


def _mixer_a_kernel(x_ref, g_ref, w_in_ref, gv_ref, ws_ref, bs_ref, w_out_ref, o_ref,
                    z_ref, v_ref, gated_ref):
    tm = x_ref.shape[0]
    x = x_ref[...]
    xn = _rms(x, g_ref[...]).astype(_BF16)
    z = _dot(xn, w_in_ref[...])
    z_ref[...] = jax.nn.gelu(z, approximate=True)
    v_ref[...] = _rms(z_ref[:, D_MODEL:], gv_ref[...]).astype(_BF16)

    row = lax.broadcasted_iota(jnp.int32, (CHUNK, CHUNK), 0)
    col = lax.broadcasted_iota(jnp.int32, (CHUNK, CHUNK), 1)
    causal = row >= col
    for h in range(A_GROUPS):
        w_h = jnp.where(causal, ws_ref[h], 0.0).astype(_BF16)
        lanes = slice(h * LANES, (h + 1) * LANES)
        for c in range(tm // CHUNK):
            rows = slice(c * CHUNK, (c + 1) * CHUNK)
            s = _dot(w_h, v_ref[rows, lanes]) + bs_ref[:, lanes]
            gated_ref[rows, lanes] = (z_ref[rows, lanes] * s).astype(_BF16)
    o_ref[...] = x + _dot(gated_ref[...], w_out_ref[...])


def _mixer_a(x, g, w_in, gv, ws, bs, w_out):
    m = x.shape[0]
    tm = TM_A
    resident = [_nbytes(w_in.shape, _BF16), _nbytes(w_out.shape, _BF16), _nbytes(ws.shape, _F32),
                _nbytes(bs.shape, _F32)]
    streamed = [2 * _nbytes((tm, D_MODEL), _F32)]
    scratch = [_nbytes((tm, 2 * D_MODEL), _F32), 2 * _nbytes((tm, D_MODEL), _BF16)]
    temporaries = [2 * _nbytes((tm, 2 * D_MODEL), _F32)]
    return pl.pallas_call(
        _mixer_a_kernel,
        out_shape=jax.ShapeDtypeStruct((m, D_MODEL), _F32),
        grid=(m // tm,),
        in_specs=[
            _row_spec(tm, D_MODEL),
            _const_spec(g.shape),
            _const_spec(w_in.shape),
            _const_spec(gv.shape),
            _const_spec(ws.shape),
            _const_spec(bs.shape),
            _const_spec(w_out.shape),
        ],
        out_specs=_row_spec(tm, D_MODEL),
        scratch_shapes=[
            pltpu.VMEM((tm, 2 * D_MODEL), _F32),
            pltpu.VMEM((tm, D_MODEL), _BF16),
            pltpu.VMEM((tm, D_MODEL), _BF16),
        ],
        compiler_params=pltpu.CompilerParams(
            dimension_semantics=("arbitrary",),
            vmem_limit_bytes=_vmem_limit(resident, streamed, scratch, temporaries)),
        name="mixer_a",
    )(x, g, w_in, gv, ws, bs, w_out)


def _ffn_ple_kernel(final_norm, h_ref, p_ref, gf_ref, wup_ref, cw_ref, cb_ref, wdown_ref,
                    gp_ref, wpi_ref, wg_ref, bg_ref, ge_ref, *rest):
    if final_norm:
        o_ref, slab_ref, hn_ref, carry_ref, act_ref = rest
    else:
        we_ref, o_ref, kv_ref, slab_ref, hn_ref, carry_ref, act_ref = rest
    tm = h_ref.shape[0]
    first_of_seq = (pl.program_id(0) % (SEQ // tm)) == 0
    n_groups = tm // GROUP
    n_slabs = D_MODEL // LANES
    chunk_tiles = FF_CHUNK // LANES
    lanes = lambda l: slice(l * LANES, (l + 1) * LANES)

    h = h_ref[...]
    hn = _rms(h, gf_ref[...])
    for j in range(n_slabs):
        slab_ref[j] = hn[:, lanes(j)]
    for j in range(n_slabs):
        for k in range(n_groups):
            for q in range(0, GROUP_TILES, 2):
                pair = [slab_ref[j, pl.ds(GROUP * k + q + d, SUBLANES, stride=GROUP_TILES), :] for d in range(2)]
                rows = slice(GROUP * k + SUBLANES * q, GROUP * k + SUBLANES * (q + 2))
                hn_ref[rows, lanes(j)] = jnp.concatenate(pair, axis=0).astype(_BF16)

    first_row = lax.broadcasted_iota(jnp.int32, (SUBLANES, LANES), 0) == 0
    tails = {}

    def conv_block(c, half, j, k):
        col = slab_ref.at[2 * chunk_tiles * (c % 2) + chunk_tiles * half + j]
        blk = col[GROUP * k:GROUP * (k + 1), :]
        prev = tails[c, half][:, lanes(j)] if k == 0 else col[GROUP * k - CARRY_ROWS:GROUP * k, :]
        start = half * D_FF + c * FF_CHUNK + j * LANES
        wl = slice(start, start + LANES)
        edge2, edge1 = (
            jnp.where(first_row, pltpu.roll(prev[s:s + SUBLANES], 1, 0),
                      pltpu.roll(blk[GROUP - 2 * SUBLANES + s:GROUP - SUBLANES + s], 1, 0))
            for s in (0, SUBLANES))
        back1 = jnp.concatenate([edge1, blk[:GROUP - SUBLANES]], axis=0)
        back2 = jnp.concatenate([edge2, edge1, blk[:GROUP - 2 * SUBLANES]], axis=0)
        w0, w1, w2 = (cw_ref[t:t + 1, wl] for t in range(CONV_WIDTH))
        return cb_ref[:, wl] + back2 * w0 + back1 * w1 + blk * w2

    for c in range(N_FF_CHUNKS + 1):
        if c < N_FF_CHUNKS:
            for half in range(2):
                start = half * D_FF + c * FF_CHUNK
                up = _dot(hn_ref[...], wup_ref[:, start:start + FF_CHUNK])
                tails[c, half] = jnp.where(first_of_seq, 0.0, carry_ref[half, c])
                carry_ref[half, c] = up[tm - CARRY_ROWS:, :]
                for j in range(chunk_tiles):
                    slab_ref[2 * chunk_tiles * (c % 2) + chunk_tiles * half + j] = up[:, lanes(j)]
        if c >= 1:
            for j in range(chunk_tiles):
                for k in range(n_groups):
                    act_ref[GROUP * k:GROUP * (k + 1), lanes((c - 1) * chunk_tiles + j)] = (
                        jax.nn.silu(conv_block(c - 1, 0, j, k)) * conv_block(c - 1, 1, j, k)).astype(_BF16)

    down = _dot(act_ref[...], wdown_ref[...])
    for j in range(n_slabs):
        for k in range(n_groups):
            for q in range(GROUP_TILES):
                rows = slice(GROUP * k + SUBLANES * q, GROUP * k + SUBLANES * (q + 1))
                slab_ref[j, pl.ds(GROUP * k + q, SUBLANES, stride=GROUP_TILES), :] = down[rows, lanes(j)]
    h = h + jnp.concatenate([slab_ref[j] for j in range(n_slabs)], axis=1)

    hn = _rms(h, gp_ref[...]).astype(_BF16)
    gate = jax.nn.sigmoid(_dot(hn, wg_ref[...]) + bg_ref[...])
    h = h + _dot(p_ref[...].astype(_BF16), wpi_ref[...]) * gate

    if final_norm:
        o_ref[...] = _rms(h, ge_ref[...])
    else:
        o_ref[...] = h
        kv_ref[...] = _dot(_rms(h, ge_ref[...]).astype(_BF16), we_ref[...]).astype(_BF16)


def _layer_spec(shape, layer):
    zeros = (0,) * (len(shape) - 1)
    return pl.BlockSpec((None,) + tuple(shape[1:]), lambda i: (layer,) + zeros, pipeline_mode=pl.Buffered(1))


def _ffn_ple(h, p, layer, stacked, ge, we=None, *, name):
    m = h.shape[0]
    tm = TM_F
    final_norm = we is None
    consts = (ge,) + (() if final_norm else (we,))
    out_shape = [jax.ShapeDtypeStruct((m, D_MODEL), _F32)]
    out_specs = [_row_spec(tm, D_MODEL)]
    streamed = [2 * _nbytes((tm, D_MODEL), _F32), _nbytes((tm, PLE_DIM), _F32)]
    if not final_norm:
        out_shape.append(jax.ShapeDtypeStruct((m, we.shape[1]), _BF16))
        out_specs.append(_row_spec(tm, we.shape[1]))
        streamed.append(_nbytes((tm, we.shape[1]), _BF16))
    scratch_shapes = [
        pltpu.VMEM((D_MODEL // LANES, tm, LANES), _F32),
        pltpu.VMEM((tm, D_MODEL), _BF16),
        pltpu.VMEM((2, N_FF_CHUNKS, CARRY_ROWS, FF_CHUNK), _F32),
        pltpu.VMEM((tm, D_FF), _BF16),
    ]
    resident = [_nbytes(w.shape[1:], w.dtype) for w in stacked] + [_nbytes(w.shape, w.dtype) for w in consts]
    scratch = [_nbytes(s.shape, s.dtype) for s in scratch_shapes]
    temporaries = [4 * _nbytes((tm, D_MODEL), _F32), 2 * _nbytes((tm, 2 * FF_CHUNK), _F32)]
    outs = pl.pallas_call(
        functools.partial(_ffn_ple_kernel, final_norm),
        out_shape=out_shape,
        grid=(m // tm,),
        in_specs=([_row_spec(tm, D_MODEL), pl.BlockSpec((None, tm, PLE_DIM), lambda i: (layer, i, 0))]
                  + [_layer_spec(w.shape, layer) for w in stacked] + [_const_spec(w.shape) for w in consts]),
        out_specs=out_specs,
        scratch_shapes=scratch_shapes,
        compiler_params=pltpu.CompilerParams(
            dimension_semantics=("arbitrary",),
            vmem_limit_bytes=_vmem_limit(resident, streamed, scratch, temporaries)),
        name=name,
    )(h, p, *stacked, *consts)
    return outs[0] if final_norm else outs


def _pair_heads(w, axis):
    shape = w.shape
    split = shape[:axis] + (N_KV_HEADS // 2, 2, GQA_GROUP, HEAD_DIM) + shape[axis + 1:]
    return jnp.swapaxes(w.reshape(split), axis + 1, axis + 2).reshape(shape)


def _attention_bias():
    i = np.arange(CHUNK)[:, None]
    j = np.arange(2 * CHUNK)[None, :]
    dist = (i + CHUNK - j).astype(np.float32)
    in_band = (dist >= 0) & (dist < CHUNK)
    slopes = np.array([2.0 ** (-8.0 * (h + 1) / N_Q_HEADS) for h in range(N_Q_HEADS)], dtype=np.float32)
    bias = -slopes[:, None, None] * dist[None] * LOG2E
    general = np.where(in_band[None], bias, -np.inf)
    first = np.where((in_band & (j >= CHUNK))[None], bias, -np.inf)
    return np.stack([general, first]).astype(np.float32)


def _mixer_b_kernel(sinks_ref, h_ref, kvc_ref, kvp_ref, g_ref, wq_ref, wo_ref, bias_ref, o_ref,
                    q_ref, att_ref):
    tm = h_ref.shape[0]
    blocks_per_seq = SEQ // CHUNK
    blocks_per_tile = tm // CHUNK
    kv_width = N_KV_HEADS * HEAD_DIM

    h = h_ref[...]
    q = _dot(_rms(h, g_ref[...]).astype(_BF16), wq_ref[...]) * (HEAD_DIM ** -0.5 * LOG2E)
    q_ref[...] = q.astype(_BF16)

    low_half = lax.broadcasted_iota(jnp.int32, (2 * CHUNK, LANES), 1) < HEAD_DIM
    low_half_q = lax.broadcasted_iota(jnp.int32, (GQA_GROUP * CHUNK, LANES), 1) < HEAD_DIM
    for b in range(blocks_per_tile):
        rows = slice(b * CHUNK, (b + 1) * CHUNK)
        block = pl.program_id(0) * blocks_per_tile + b
        first = jnp.where(block % blocks_per_seq == 0, 1, 0)
        if b == 0:
            kv_prev = kvp_ref[...]
        else:
            kv_prev = kvc_ref[(b - 1) * CHUNK:b * CHUNK, :]
        kv = jnp.concatenate([kv_prev, kvc_ref[rows, :]], axis=0)
        for pair in range(N_KV_HEADS // 2):
            k_pair = kv[:, pair * LANES:(pair + 1) * LANES]
            v_pair = kv[:, kv_width + pair * LANES:kv_width + (pair + 1) * LANES]
            groups = [pair * GQA_GROUP + g for g in range(GQA_GROUP)]
            q_stack = jnp.concatenate([q_ref[rows, j * LANES:(j + 1) * LANES] for j in groups], axis=0)
            weights, recips = [], []
            for half in range(2):
                kv_head = 2 * pair + half
                k_half = jnp.where(low_half if half == 0 else ~low_half, k_pair, 0)
                s = lax.dot_general(q_stack, k_half, (((1,), (1,)), ((), ())),
                                    preferred_element_type=_F32)
                parts, rparts = [], []
                for g in range(GQA_GROUP):
                    head = kv_head * GQA_GROUP + g
                    sg = s[g * CHUNK:(g + 1) * CHUNK, :] + bias_ref[first, head]
                    sink = jnp.full((CHUNK, 1), sinks_ref[head], _F32) * LOG2E
                    mx = jnp.maximum(jnp.max(sg, axis=-1, keepdims=True), sink)
                    e = jnp.exp2(sg - mx)
                    denom = jnp.sum(e, axis=-1, keepdims=True) + jnp.exp2(sink - mx)
                    parts.append(e.astype(_BF16))
                    rparts.append(1.0 / denom)
                weights.append(jnp.concatenate(parts, axis=0))
                recips.append(jnp.concatenate(rparts, axis=0))
            p_cat = jnp.concatenate(weights, axis=1)
            v_cat = jnp.concatenate([jnp.where(low_half, v_pair, 0), jnp.where(low_half, 0, v_pair)], axis=0)
            out = _dot(p_cat, v_cat) * jnp.where(low_half_q, recips[0], recips[1])
            for g in range(GQA_GROUP):
                j = groups[g]
                att_ref[rows, j * LANES:(j + 1) * LANES] = out[g * CHUNK:(g + 1) * CHUNK, :].astype(_BF16)
    o_ref[...] = h + _dot(att_ref[...], wo_ref[...])


def _mixer_b(h, kv, g, wq, wo, sinks, bias):
    m = h.shape[0]
    tm = TM_B
    blocks_per_tile = tm // CHUNK
    kvw = kv.shape[1]
    resident = [_nbytes(wq.shape, _BF16), _nbytes(wo.shape, _BF16), _nbytes(bias.shape, _F32)]
    streamed = [2 * _nbytes((tm, D_MODEL), _F32), _nbytes((tm + CHUNK, kvw), _BF16)]
    scratch = [2 * _nbytes((tm, D_MODEL), _BF16)]
    temporaries = [3 * _nbytes((tm, D_MODEL), _F32), 8 * _nbytes((GQA_GROUP * CHUNK, 2 * CHUNK), _F32)]

    return pl.pallas_call(
        _mixer_b_kernel,
        out_shape=jax.ShapeDtypeStruct((m, D_MODEL), _F32),
        grid=(m // tm,),
        in_specs=[
            pl.BlockSpec(memory_space=pltpu.SMEM),
            _row_spec(tm, D_MODEL),
            _row_spec(tm, kvw),
            pl.BlockSpec((CHUNK, kvw), lambda i: (jnp.maximum(i * blocks_per_tile - 1, 0), 0)),
            _const_spec(g.shape),
            _const_spec(wq.shape),
            _const_spec(wo.shape),
            _const_spec(bias.shape),
        ],
        out_specs=_row_spec(tm, D_MODEL),
        scratch_shapes=[pltpu.VMEM((tm, D_MODEL), _BF16), pltpu.VMEM((tm, D_MODEL), _BF16)],
        compiler_params=pltpu.CompilerParams(
            dimension_semantics=("arbitrary",),
            vmem_limit_bytes=_vmem_limit(resident, streamed, scratch, temporaries)),
        name="mixer_b",
    )(sinks, h, kv, kv, g, wq, wo, bias)


def kernel(x, p, norm_mix, norm_ffn, norm_ple, norm_kv, norm_final, a_w_in, a_norm_v, a_w_s, a_b_s, a_w_out, w_kv, b_w_q, b_sinks, b_w_o, f_w_up, f_conv_w, f_conv_b, f_w_down, ple_w_in, ple_w_gate, ple_b_gate):
    batch, seq, d = x.shape
    assert (seq, d) == (SEQ, D_MODEL) and seq % max(TM_A, TM_F, TM_B) == 0
    m = batch * seq
    row = lambda a: a.reshape(1, -1)

    h = x.reshape(m, d)
    bias_s = jnp.repeat(a_b_s[0].T, LANES, axis=1)
    h = _mixer_a(h, row(norm_mix[0]), a_w_in[0].astype(_BF16), row(a_norm_v[0]), a_w_s[0], bias_s,
                 a_w_out[0].astype(_BF16))

    rows = lambda a: a[:, None, :]
    stacked = (rows(norm_ffn), f_w_up.astype(_BF16), f_conv_w, rows(f_conv_b), f_w_down.astype(_BF16),
               rows(norm_ple), ple_w_in.astype(_BF16), ple_w_gate.astype(_BF16), rows(ple_b_gate))
    p = p.reshape(p.shape[0], m, PLE_DIM)

    h, kv = _ffn_ple(h, p, 0, stacked, row(norm_kv), w_kv.astype(_BF16), name="ffn_ple_0")

    h = _mixer_b(h, kv, row(norm_mix[1]), _pair_heads(b_w_q[0], 1).astype(_BF16),
                 _pair_heads(b_w_o[0], 0).astype(_BF16), b_sinks[0], jnp.asarray(_attention_bias()))

    out = _ffn_ple(h, p, 1, stacked, row(norm_final), name="ffn_ple_1")
    return out.reshape(batch, seq, d)
```

```python
import functools
import math

import jax
import jax.numpy as jnp
import numpy as np
from jax import lax
from jax.experimental import pallas as pl
from jax.experimental.pallas import tpu as pltpu

D_MODEL = 1024
SEQ = 8192
CHUNK = 128
A_GROUPS = 8
HEAD_DIM = 64
N_Q_HEADS = 16
N_KV_HEADS = 4
GQA_GROUP = N_Q_HEADS // N_KV_HEADS
D_FF = 2816
CONV_WIDTH = 3
PLE_DIM = 256
EPS = 1e-6

LANES = 128
SUBLANES = 8
MXU_WIDTH = 256
V7X_VMEM_BYTES = 64 * 1024 * 1024

TM_A = 1024
TM_F = 1024
TM_B = 1024
FF_CHUNK = MXU_WIDTH
N_FF_CHUNKS = D_FF // FF_CHUNK
GROUP_TILES = 4
GROUP = SUBLANES * GROUP_TILES
CARRY_ROWS = (CONV_WIDTH - 1) * SUBLANES

_BF16 = jnp.bfloat16
_F32 = jnp.float32


def _dot(a, b):
    return jnp.dot(a, b, preferred_element_type=_F32)


def _rms(x, g):
    return x * lax.rsqrt(jnp.mean(x * x, axis=-1, keepdims=True) + EPS) * g


def _const_spec(shape):
    zeros = (0,) * len(shape)
    return pl.BlockSpec(shape, lambda i: zeros, pipeline_mode=pl.Buffered(1))


def _row_spec(tm, width):
    return pl.BlockSpec((tm, width), lambda i: (i, 0))


def _nbytes(shape, dtype):
    return math.prod(shape) * jnp.dtype(dtype).itemsize


def _vmem_limit(resident, streamed, scratch, temporaries):
    total = sum(resident) + 2 * sum(streamed) + sum(scratch) + sum(temporaries)
    return min(int(total), V7X_VMEM_BYTES)


def _mixer_a_kernel(x_ref, g_ref, w_in_ref, gv_ref, ws_ref, bs_ref, w_out_ref, o_ref,
                    z_ref, v_ref, gated_ref):
    tm = x_ref.shape[0]
    x = x_ref[...]
    xn = _rms(x, g_ref[...]).astype(_BF16)
    z = _dot(xn, w_in_ref[...])
    z_ref[...] = jax.nn.gelu(z, approximate=True)
    v_ref[...] = _rms(z_ref[:, D_MODEL:], gv_ref[...]).astype(_BF16)

    row = lax.broadcasted_iota(jnp.int32, (CHUNK, CHUNK), 0)
    col = lax.broadcasted_iota(jnp.int32, (CHUNK, CHUNK), 1)
    causal = row >= col
    for h in range(A_GROUPS):
        w_h = jnp.where(causal, ws_ref[h], 0.0).astype(_BF16)
        lanes = slice(h * LANES, (h + 1) * LANES)
        for c in range(tm // CHUNK):
            rows = slice(c * CHUNK, (c + 1) * CHUNK)
            s = _dot(w_h, v_ref[rows, lanes]) + bs_ref[:, lanes]
            gated_ref[rows, lanes] = (z_ref[rows, lanes] * s).astype(_BF16)
    o_ref[...] = x + _dot(gated_ref[...], w_out_ref[...])


def _mixer_a(x, g, w_in, gv, ws, bs, w_out):
    m = x.shape[0]
    tm = TM_A
    resident = [_nbytes(w_in.shape, _BF16), _nbytes(w_out.shape, _BF16), _nbytes(ws.shape, _F32),
                _nbytes(bs.shape, _F32)]
    streamed = [2 * _nbytes((tm, D_MODEL), _F32)]
    scratch = [_nbytes((tm, 2 * D_MODEL), _F32), 2 * _nbytes((tm, D_MODEL), _BF16)]
    temporaries = [2 * _nbytes((tm, 2 * D_MODEL), _F32)]
    return pl.pallas_call(
        _mixer_a_kernel,
        out_shape=jax.ShapeDtypeStruct((m, D_MODEL), _F32),
        grid=(m // tm,),
        in_specs=[
            _row_spec(tm, D_MODEL),
            _const_spec(g.shape),
            _const_spec(w_in.shape),
            _const_spec(gv.shape),
            _const_spec(ws.shape),
            _const_spec(bs.shape),
            _const_spec(w_out.shape),
        ],
        out_specs=_row_spec(tm, D_MODEL),
        scratch_shapes=[
            pltpu.VMEM((tm, 2 * D_MODEL), _F32),
            pltpu.VMEM((tm, D_MODEL), _BF16),
            pltpu.VMEM((tm, D_MODEL), _BF16),
        ],
        compiler_params=pltpu.CompilerParams(
            dimension_semantics=("arbitrary",),
            vmem_limit_bytes=_vmem_limit(resident, streamed, scratch, temporaries)),
        name="mixer_a",
    )(x, g, w_in, gv, ws, bs, w_out)


def _ffn_ple_kernel(final_norm, h_ref, p_ref, gf_ref, wup_ref, cw_ref, cb_ref, wdown_ref,
                    gp_ref, wpi_ref, wg_ref, bg_ref, ge_ref, *rest):
    if final_norm:
        o_ref, slab_ref, hn_ref, carry_ref, act_ref = rest
    else:
        we_ref, o_ref, kv_ref, slab_ref, hn_ref, carry_ref, act_ref = rest
    tm = h_ref.shape[0]
    first_of_seq = (pl.program_id(0) % (SEQ // tm)) == 0
    n_groups = tm // GROUP
    n_slabs = D_MODEL // LANES
    chunk_tiles = FF_CHUNK // LANES
    lanes = lambda l: slice(l * LANES, (l + 1) * LANES)

    h = h_ref[...]
    hn = _rms(h, gf_ref[...])
    for j in range(n_slabs):
        slab_ref[j] = hn[:, lanes(j)]
    for j in range(n_slabs):
        for k in range(n_groups):
            for q in range(0, GROUP_TILES, 2):
                pair = [slab_ref[j, pl.ds(GROUP * k + q + d, SUBLANES, stride=GROUP_TILES), :] for d in range(2)]
                rows = slice(GROUP * k + SUBLANES * q, GROUP * k + SUBLANES * (q + 2))
                hn_ref[rows, lanes(j)] = jnp.concatenate(pair, axis=0).astype(_BF16)

    first_row = lax.broadcasted_iota(jnp.int32, (SUBLANES, LANES), 0) == 0
    tails = {}

    def conv_block(c, half, j, k):
        col = slab_ref.at[2 * chunk_tiles * (c % 2) + chunk_tiles * half + j]
        blk = col[GROUP * k:GROUP * (k + 1), :]
        prev = tails[c, half][:, lanes(j)] if k == 0 else col[GROUP * k - CARRY_ROWS:GROUP * k, :]
        start = half * D_FF + c * FF_CHUNK + j * LANES
        wl = slice(start, start + LANES)
        edge2, edge1 = (
            jnp.where(first_row, pltpu.roll(prev[s:s + SUBLANES], 1, 0),
                      pltpu.roll(blk[GROUP - 2 * SUBLANES + s:GROUP - SUBLANES + s], 1, 0))
            for s in (0, SUBLANES))
        back1 = jnp.concatenate([edge1, blk[:GROUP - SUBLANES]], axis=0)
        back2 = jnp.concatenate([edge2, edge1, blk[:GROUP - 2 * SUBLANES]], axis=0)
        w0, w1, w2 = (cw_ref[t:t + 1, wl] for t in range(CONV_WIDTH))
        return cb_ref[:, wl] + back2 * w0 + back1 * w1 + blk * w2

    for c in range(N_FF_CHUNKS + 1):
        if c < N_FF_CHUNKS:
            for half in range(2):
                start = half * D_FF + c * FF_CHUNK
                up = _dot(hn_ref[...], wup_ref[:, start:start + FF_CHUNK])
                tails[c, half] = jnp.where(first_of_seq, 0.0, carry_ref[half, c])
                carry_ref[half, c] = up[tm - CARRY_ROWS:, :]
                for j in range(chunk_tiles):
                    slab_ref[2 * chunk_tiles * (c % 2) + chunk_tiles * half + j] = up[:, lanes(j)]
        if c >= 1:
            for j in range(chunk_tiles):
                for k in range(n_groups):
                    act_ref[GROUP * k:GROUP * (k + 1), lanes((c - 1) * chunk_tiles + j)] = (
                        jax.nn.silu(conv_block(c - 1, 0, j, k)) * conv_block(c - 1, 1, j, k)).astype(_BF16)

    down = _dot(act_ref[...], wdown_ref[...])
    for j in range(n_slabs):
        for k in range(n_groups):
            for q in range(GROUP_TILES):
                rows = slice(GROUP * k + SUBLANES * q, GROUP * k + SUBLANES * (q + 1))
                slab_ref[j, pl.ds(GROUP * k + q, SUBLANES, stride=GROUP_TILES), :] = down[rows, lanes(j)]
    h = h + jnp.concatenate([slab_ref[j] for j in range(n_slabs)], axis=1)

    hn = _rms(h, gp_ref[...]).astype(_BF16)
    gate = jax.nn.sigmoid(_dot(hn, wg_ref[...]) + bg_ref[...])
    h = h + _dot(p_ref[...].astype(_BF16), wpi_ref[...]) * gate

    if final_norm:
        o_ref[...] = _rms(h, ge_ref[...])
    else:
        o_ref[...] = h
        kv_ref[...] = _dot(_rms(h, ge_ref[...]).astype(_BF16), we_ref[...]).astype(_BF16)


def _layer_spec(shape, layer):
    zeros = (0,) * (len(shape) - 1)
    return pl.BlockSpec((None,) + tuple(shape[1:]), lambda i: (layer,) + zeros, pipeline_mode=pl.Buffered(1))


def _ffn_ple(h, p, layer, stacked, ge, we=None, *, name):
    m = h.shape[0]
    tm = TM_F
    final_norm = we is None
    consts = (ge,) + (() if final_norm else (we,))
    out_shape = [jax.ShapeDtypeStruct((m, D_MODEL), _F32)]
    out_specs = [_row_spec(tm, D_MODEL)]
    streamed = [2 * _nbytes((tm, D_MODEL), _F32), _nbytes((tm, PLE_DIM), _F32)]
    if not final_norm:
        out_shape.append(jax.ShapeDtypeStruct((m, we.shape[1]), _BF16))
        out_specs.append(_row_spec(tm, we.shape[1]))
        streamed.append(_nbytes((tm, we.shape[1]), _BF16))
    scratch_shapes = [
        pltpu.VMEM((D_MODEL // LANES, tm, LANES), _F32),
        pltpu.VMEM((tm, D_MODEL), _BF16),
        pltpu.VMEM((2, N_FF_CHUNKS, CARRY_ROWS, FF_CHUNK), _F32),
        pltpu.VMEM((tm, D_FF), _BF16),
    ]
    resident = [_nbytes(w.shape[1:], w.dtype) for w in stacked] + [_nbytes(w.shape, w.dtype) for w in consts]
    scratch = [_nbytes(s.shape, s.dtype) for s in scratch_shapes]
    temporaries = [4 * _nbytes((tm, D_MODEL), _F32), 2 * _nbytes((tm, 2 * FF_CHUNK), _F32)]
    outs = pl.pallas_call(
        functools.partial(_ffn_ple_kernel, final_norm),
        out_shape=out_shape,
        grid=(m // tm,),
        in_specs=([_row_spec(tm, D_MODEL), pl.BlockSpec((None, tm, PLE_DIM), lambda i: (layer, i, 0))]
                  + [_layer_spec(w.shape, layer) for w in stacked] + [_const_spec(w.shape) for w in consts]),
        out_specs=out_specs,
        scratch_shapes=scratch_shapes,
        compiler_params=pltpu.CompilerParams(
            dimension_semantics=("arbitrary",),
            vmem_limit_bytes=_vmem_limit(resident, streamed, scratch, temporaries)),
        name=name,
    )(h, p, *stacked, *consts)
    return outs[0] if final_norm else outs


def _pair_heads(w, axis):
    shape = w.shape
    split = shape[:axis] + (N_KV_HEADS // 2, 2, GQA_GROUP, HEAD_DIM) + shape[axis + 1:]
    return jnp.swapaxes(w.reshape(split), axis + 1, axis + 2).reshape(shape)


def _attention_bias():
    i = np.arange(CHUNK)[:, None]
    j = np.arange(2 * CHUNK)[None, :]
    dist = (i + CHUNK - j).astype(np.float32)
    in_band = (dist >= 0) & (dist < CHUNK)
    slopes = np.array([2.0 ** (-8.0 * (h + 1) / N_Q_HEADS) for h in range(N_Q_HEADS)], dtype=np.float32)
    bias = -slopes[:, None, None] * dist[None]
    general = np.where(in_band[None], bias, -np.inf)
    first = np.where((in_band & (j >= CHUNK))[None], bias, -np.inf)
    return np.stack([general, first]).astype(np.float32)


def _mixer_b_kernel(sinks_ref, h_ref, kvc_ref, kvp_ref, g_ref, wq_ref, wo_ref, bias_ref, o_ref,
                    q_ref, att_ref):
    tm = h_ref.shape[0]
    blocks_per_seq = SEQ // CHUNK
    blocks_per_tile = tm // CHUNK
    kv_width = N_KV_HEADS * HEAD_DIM

    h = h_ref[...]
    q = _dot(_rms(h, g_ref[...]).astype(_BF16), wq_ref[...]) * (HEAD_DIM ** -0.5)
    q_ref[...] = q.astype(_BF16)

    lane = lax.broadcasted_iota(jnp.int32, (2 * CHUNK, LANES), 1)
    low_half = lane < HEAD_DIM
    for b in range(blocks_per_tile):
        rows = slice(b * CHUNK, (b + 1) * CHUNK)
        block = pl.program_id(0) * blocks_per_tile + b
        first = jnp.where(block % blocks_per_seq == 0, 1, 0)
        if b == 0:
            kv_prev = kvp_ref[...]
        else:
            kv_prev = kvc_ref[(b - 1) * CHUNK:b * CHUNK, :]
        kv = jnp.concatenate([kv_prev, kvc_ref[rows, :]], axis=0)
        for pair in range(N_KV_HEADS // 2):
            k_pair = kv[:, pair * LANES:(pair + 1) * LANES]
            v_pair = kv[:, kv_width + pair * LANES:kv_width + (pair + 1) * LANES]
            groups = [pair * GQA_GROUP + g for g in range(GQA_GROUP)]
            q_stack = jnp.concatenate([q_ref[rows, j * LANES:(j + 1) * LANES] for j in groups], axis=0)
            probs = []
            for half in range(2):
                kv_head = 2 * pair + half
                k_half = jnp.where(low_half if half == 0 else ~low_half, k_pair, 0)
                s = lax.dot_general(q_stack, k_half, (((1,), (1,)), ((), ())),
                                    preferred_element_type=_F32)
                parts = []
                for g in range(GQA_GROUP):
                    head = kv_head * GQA_GROUP + g
                    sg = s[g * CHUNK:(g + 1) * CHUNK, :] + bias_ref[first, head]
                    sink = sinks_ref[head]
                    mx = jnp.maximum(jnp.max(sg, axis=-1, keepdims=True), sink)
                    e = jnp.exp(sg - mx)
                    denom = jnp.sum(e, axis=-1, keepdims=True) + jnp.exp(sink - mx)
                    parts.append((e * (1.0 / denom)).astype(_BF16))
                probs.append(jnp.concatenate(parts, axis=0))
            p_cat = jnp.concatenate(probs, axis=1)
            v_cat = jnp.concatenate([jnp.where(low_half, v_pair, 0), jnp.where(low_half, 0, v_pair)], axis=0)
            out = _dot(p_cat, v_cat)
            for g in range(GQA_GROUP):
                j = groups[g]
                att_ref[rows, j * LANES:(j + 1) * LANES] = out[g * CHUNK:(g + 1) * CHUNK, :].astype(_BF16)
    o_ref[...] = h + _dot(att_ref[...], wo_ref[...])


def _mixer_b(h, kv, g, wq, wo, sinks, bias):
    m = h.shape[0]
    tm = TM_B
    blocks_per_tile = tm // CHUNK
    kvw = kv.shape[1]
    resident = [_nbytes(wq.shape, _BF16), _nbytes(wo.shape, _BF16), _nbytes(bias.shape, _F32)]
    streamed = [2 * _nbytes((tm, D_MODEL), _F32), _nbytes((tm + CHUNK, kvw), _BF16)]
    scratch = [2 * _nbytes((tm, D_MODEL), _BF16)]
    temporaries = [3 * _nbytes((tm, D_MODEL), _F32), 8 * _nbytes((GQA_GROUP * CHUNK, 2 * CHUNK), _F32)]

    return pl.pallas_call(
        _mixer_b_kernel,
        out_shape=jax.ShapeDtypeStruct((m, D_MODEL), _F32),
        grid=(m // tm,),
        in_specs=[
            pl.BlockSpec(memory_space=pltpu.SMEM),
            _row_spec(tm, D_MODEL),
            _row_spec(tm, kvw),
            pl.BlockSpec((CHUNK, kvw), lambda i: (jnp.maximum(i * blocks_per_tile - 1, 0), 0)),
            _const_spec(g.shape),
            _const_spec(wq.shape),
            _const_spec(wo.shape),
            _const_spec(bias.shape),
        ],
        out_specs=_row_spec(tm, D_MODEL),
        scratch_shapes=[pltpu.VMEM((tm, D_MODEL), _BF16), pltpu.VMEM((tm, D_MODEL), _BF16)],
        compiler_params=pltpu.CompilerParams(
            dimension_semantics=("arbitrary",),
            vmem_limit_bytes=_vmem_limit(resident, streamed, scratch, temporaries)),
        name="mixer_b",
    )(sinks, h, kv, kv, g, wq, wo, bias)


def kernel(x, p, norm_mix, norm_ffn, norm_ple, norm_kv, norm_final, a_w_in, a_norm_v, a_w_s, a_b_s, a_w_out, w_kv, b_w_q, b_sinks, b_w_o, f_w_up, f_conv_w, f_conv_b, f_w_down, ple_w_in, ple_w_gate, ple_b_gate):
    batch, seq, d = x.shape
    assert (seq, d) == (SEQ, D_MODEL) and seq % max(TM_A, TM_F, TM_B) == 0
    m = batch * seq
    row = lambda a: a.reshape(1, -1)

    h = x.reshape(m, d)
    bias_s = jnp.repeat(a_b_s[0].T, LANES, axis=1)
    h = _mixer_a(h, row(norm_mix[0]), a_w_in[0].astype(_BF16), row(a_norm_v[0]), a_w_s[0], bias_s,
                 a_w_out[0].astype(_BF16))

    rows = lambda a: a[:, None, :]
    stacked = (rows(norm_ffn), f_w_up.astype(_BF16), f_conv_w, rows(f_conv_b), f_w_down.astype(_BF16),
               rows(norm_ple), ple_w_in.astype(_BF16), ple_w_gate.astype(_BF16), rows(ple_b_gate))
    p = p.reshape(p.shape[0], m, PLE_DIM)

    h, kv = _ffn_ple(h, p, 0, stacked, row(norm_kv), w_kv.astype(_BF16), name="ffn_ple_0")

    h = _mixer_b(h, kv, row(norm_mix[1]), _pair_heads(b_w_q[0], 1).astype(_BF16),
                 _pair_heads(b_w_o[0], 0).astype(_BF16), b_sinks[0], jnp.asarray(_attention_bias()))

    out = _ffn_ple(h, p, 1, stacked, row(norm_final), name="ffn_ple_1")
    return out.reshape(batch, seq, d)
```

```python
import functools
import math

import jax
import jax.numpy as jnp
import numpy as np
from jax import lax
from jax.experimental import pallas as pl
from jax.experimental.pallas import tpu as pltpu

D_MODEL = 1024
SEQ = 8192
CHUNK = 128
A_GROUPS = 8
HEAD_DIM = 64
N_Q_HEADS = 16
N_KV_HEADS = 4
GQA_GROUP = N_Q_HEADS // N_KV_HEADS
D_FF = 2816
CONV_WIDTH = 3
PLE_DIM = 256
EPS = 1e-6

LANES = 128
SUBLANES = 8
MXU_WIDTH = 256
V7X_VMEM_BYTES = 64 * 1024 * 1024

TM_A = 1024
TM_F = 1024
TM_B = 1024
FF_CHUNK = MXU_WIDTH
N_FF_CHUNKS = D_FF // FF_CHUNK
GROUP_TILES = 8
GROUP = SUBLANES * GROUP_TILES
CARRY_ROWS = (CONV_WIDTH - 1) * SUBLANES

_BF16 = jnp.bfloat16
_F32 = jnp.float32


def _dot(a, b):
    return jnp.dot(a, b, preferred_element_type=_F32)


def _rms(x, g):
    return x * lax.rsqrt(jnp.mean(x * x, axis=-1, keepdims=True) + EPS) * g


def _const_spec(shape):
    zeros = (0,) * len(shape)
    return pl.BlockSpec(shape, lambda i: zeros, pipeline_mode=pl.Buffered(1))


def _row_spec(tm, width):
    return pl.BlockSpec((tm, width), lambda i: (i, 0))


def _nbytes(shape, dtype):
    return math.prod(shape) * jnp.dtype(dtype).itemsize


def _vmem_limit(resident, streamed, scratch, temporaries):
    total = sum(resident) + 2 * sum(streamed) + sum(scratch) + sum(temporaries)
    return min(int(total), V7X_VMEM_BYTES)


def _mixer_a_kernel(x_ref, g_ref, w_in_ref, gv_ref, ws_ref, bs_ref, w_out_ref, o_ref,
                    z_ref, v_ref, gated_ref):
    tm = x_ref.shape[0]
    x = x_ref[...]
    xn = _rms(x, g_ref[...]).astype(_BF16)
    z = _dot(xn, w_in_ref[...])
    z_ref[...] = jax.nn.gelu(z, approximate=True)
    v_ref[...] = _rms(z_ref[:, D_MODEL:], gv_ref[...]).astype(_BF16)

    row = lax.broadcasted_iota(jnp.int32, (CHUNK, CHUNK), 0)
    col = lax.broadcasted_iota(jnp.int32, (CHUNK, CHUNK), 1)
    causal = row >= col
    for h in range(A_GROUPS):
        w_h = jnp.where(causal, ws_ref[h], 0.0).astype(_BF16)
        lanes = slice(h * LANES, (h + 1) * LANES)
        for c in range(tm // CHUNK):
            rows = slice(c * CHUNK, (c + 1) * CHUNK)
            s = _dot(w_h, v_ref[rows, lanes]) + bs_ref[:, lanes]
            gated_ref[rows, lanes] = (z_ref[rows, lanes] * s).astype(_BF16)
    o_ref[...] = x + _dot(gated_ref[...], w_out_ref[...])


def _mixer_a(x, g, w_in, gv, ws, bs, w_out):
    m = x.shape[0]
    tm = TM_A
    resident = [_nbytes(w_in.shape, _BF16), _nbytes(w_out.shape, _BF16), _nbytes(ws.shape, _F32),
                _nbytes(bs.shape, _F32)]
    streamed = [2 * _nbytes((tm, D_MODEL), _F32)]
    scratch = [_nbytes((tm, 2 * D_MODEL), _F32), 2 * _nbytes((tm, D_MODEL), _BF16)]
    temporaries = [2 * _nbytes((tm, 2 * D_MODEL), _F32)]
    return pl.pallas_call(
        _mixer_a_kernel,
        out_shape=jax.ShapeDtypeStruct((m, D_MODEL), _F32),
        grid=(m // tm,),
        in_specs=[
            _row_spec(tm, D_MODEL),
            _const_spec(g.shape),
            _const_spec(w_in.shape),
            _const_spec(gv.shape),
            _const_spec(ws.shape),
            _const_spec(bs.shape),
            _const_spec(w_out.shape),
        ],
        out_specs=_row_spec(tm, D_MODEL),
        scratch_shapes=[
            pltpu.VMEM((tm, 2 * D_MODEL), _F32),
            pltpu.VMEM((tm, D_MODEL), _BF16),
            pltpu.VMEM((tm, D_MODEL), _BF16),
        ],
        compiler_params=pltpu.CompilerParams(
            dimension_semantics=("arbitrary",),
            vmem_limit_bytes=_vmem_limit(resident, streamed, scratch, temporaries)),
        name="mixer_a",
    )(x, g, w_in, gv, ws, bs, w_out)


def _ffn_ple_kernel(final_norm, h_ref, p_ref, gf_ref, wup_ref, cw_ref, cb_ref, wdown_ref,
                    gp_ref, wpi_ref, wg_ref, bg_ref, ge_ref, *rest):
    if final_norm:
        o_ref, slab_ref, hn_ref, carry_ref, act_ref = rest
    else:
        we_ref, o_ref, kv_ref, slab_ref, hn_ref, carry_ref, act_ref = rest
    tm = h_ref.shape[0]
    first_of_seq = (pl.program_id(0) % (SEQ // tm)) == 0
    n_groups = tm // GROUP
    n_slabs = D_MODEL // LANES
    chunk_tiles = FF_CHUNK // LANES
    lanes = lambda l: slice(l * LANES, (l + 1) * LANES)

    h = h_ref[...]
    hn = _rms(h, gf_ref[...])
    for j in range(n_slabs):
        slab_ref[j] = hn[:, lanes(j)]
    for j in range(n_slabs):
        for k in range(n_groups):
            for q in range(0, GROUP_TILES, 2):
                pair = [slab_ref[j, pl.ds(GROUP * k + q + d, SUBLANES, stride=GROUP_TILES), :] for d in range(2)]
                rows = slice(GROUP * k + SUBLANES * q, GROUP * k + SUBLANES * (q + 2))
                hn_ref[rows, lanes(j)] = jnp.concatenate(pair, axis=0).astype(_BF16)

    first_row = lax.broadcasted_iota(jnp.int32, (SUBLANES, LANES), 0) == 0
    tails = {}

    def conv_block(c, half, j, k):
        col = slab_ref.at[2 * chunk_tiles * (c % 2) + chunk_tiles * half + j]
        blk = col[GROUP * k:GROUP * (k + 1), :]
        prev = tails[c, half][:, lanes(j)] if k == 0 else col[GROUP * k - CARRY_ROWS:GROUP * k, :]
        start = half * D_FF + c * FF_CHUNK + j * LANES
        wl = slice(start, start + LANES)
        edge2, edge1 = (
            jnp.where(first_row, pltpu.roll(prev[s:s + SUBLANES], 1, 0),
                      pltpu.roll(blk[GROUP - 2 * SUBLANES + s:GROUP - SUBLANES + s], 1, 0))
            for s in (0, SUBLANES))
        back1 = jnp.concatenate([edge1, blk[:GROUP - SUBLANES]], axis=0)
        back2 = jnp.concatenate([edge2, edge1, blk[:GROUP - 2 * SUBLANES]], axis=0)
        w0, w1, w2 = (cw_ref[t:t + 1, wl] for t in range(CONV_WIDTH))
        return cb_ref[:, wl] + back2 * w0 + back1 * w1 + blk * w2

    for c in range(N_FF_CHUNKS + 1):
        if c < N_FF_CHUNKS:
            for half in range(2):
                start = half * D_FF + c * FF_CHUNK
                up = _dot(hn_ref[...], wup_ref[:, start:start + FF_CHUNK])
                tails[c, half] = jnp.where(first_of_seq, 0.0, carry_ref[half, c])
                carry_ref[half, c] = up[tm - CARRY_ROWS:, :]
                for j in range(chunk_tiles):
                    slab_ref[2 * chunk_tiles * (c % 2) + chunk_tiles * half + j] = up[:, lanes(j)]
        if c >= 1:
            for j in range(chunk_tiles):
                for k in range(n_groups):
                    act_ref[GROUP * k:GROUP * (k + 1), lanes((c - 1) * chunk_tiles + j)] = (
                        jax.nn.silu(conv_block(c - 1, 0, j, k)) * conv_block(c - 1, 1, j, k)).astype(_BF16)

    down = _dot(act_ref[...], wdown_ref[...])
    for j in range(n_slabs):
        for k in range(n_groups):
            for q in range(GROUP_TILES):
                rows = slice(GROUP * k + SUBLANES * q, GROUP * k + SUBLANES * (q + 1))
                slab_ref[j, pl.ds(GROUP * k + q, SUBLANES, stride=GROUP_TILES), :] = down[rows, lanes(j)]
    h = h + jnp.concatenate([slab_ref[j] for j in range(n_slabs)], axis=1)

    hn = _rms(h, gp_ref[...]).astype(_BF16)
    gate = jax.nn.sigmoid(_dot(hn, wg_ref[...]) + bg_ref[...])
    h = h + _dot(p_ref[...].astype(_BF16), wpi_ref[...]) * gate

    if final_norm:
        o_ref[...] = _rms(h, ge_ref[...])
    else:
        o_ref[...] = h
        kv_ref[...] = _dot(_rms(h, ge_ref[...]).astype(_BF16), we_ref[...]).astype(_BF16)


def _layer_spec(shape, layer):
    zeros = (0,) * (len(shape) - 1)
    return pl.BlockSpec((None,) + tuple(shape[1:]), lambda i: (layer,) + zeros, pipeline_mode=pl.Buffered(1))


def _ffn_ple(h, p, layer, stacked, ge, we=None, *, name):
    m = h.shape[0]
    tm = TM_F
    final_norm = we is None
    consts = (ge,) + (() if final_norm else (we,))
    out_shape = [jax.ShapeDtypeStruct((m, D_MODEL), _F32)]
    out_specs = [_row_spec(tm, D_MODEL)]
    streamed = [2 * _nbytes((tm, D_MODEL), _F32), _nbytes((tm, PLE_DIM), _F32)]
    if not final_norm:
        out_shape.append(jax.ShapeDtypeStruct((m, we.shape[1]), _BF16))
        out_specs.append(_row_spec(tm, we.shape[1]))
        streamed.append(_nbytes((tm, we.shape[1]), _BF16))
    scratch_shapes = [
        pltpu.VMEM((D_MODEL // LANES, tm, LANES), _F32),
        pltpu.VMEM((tm, D_MODEL), _BF16),
        pltpu.VMEM((2, N_FF_CHUNKS, CARRY_ROWS, FF_CHUNK), _F32),
        pltpu.VMEM((tm, D_FF), _BF16),
    ]
    resident = [_nbytes(w.shape[1:], w.dtype) for w in stacked] + [_nbytes(w.shape, w.dtype) for w in consts]
    scratch = [_nbytes(s.shape, s.dtype) for s in scratch_shapes]
    temporaries = [4 * _nbytes((tm, D_MODEL), _F32), 2 * _nbytes((tm, 2 * FF_CHUNK), _F32)]
    outs = pl.pallas_call(
        functools.partial(_ffn_ple_kernel, final_norm),
        out_shape=out_shape,
        grid=(m // tm,),
        in_specs=([_row_spec(tm, D_MODEL), pl.BlockSpec((None, tm, PLE_DIM), lambda i: (layer, i, 0))]
                  + [_layer_spec(w.shape, layer) for w in stacked] + [_const_spec(w.shape) for w in consts]),
        out_specs=out_specs,
        scratch_shapes=scratch_shapes,
        compiler_params=pltpu.CompilerParams(
            dimension_semantics=("arbitrary",),
            vmem_limit_bytes=_vmem_limit(resident, streamed, scratch, temporaries)),
        name=name,
    )(h, p, *stacked, *consts)
    return outs[0] if final_norm else outs


def _pair_heads(w, axis):
    shape = w.shape
    split = shape[:axis] + (N_KV_HEADS // 2, 2, GQA_GROUP, HEAD_DIM) + shape[axis + 1:]
    return jnp.swapaxes(w.reshape(split), axis + 1, axis + 2).reshape(shape)


def _attention_bias():
    i = np.arange(CHUNK)[:, None]
    j = np.arange(2 * CHUNK)[None, :]
    dist = (i + CHUNK - j).astype(np.float32)
    in_band = (dist >= 0) & (dist < CHUNK)
    slopes = np.array([2.0 ** (-8.0 * (h + 1) / N_Q_HEADS) for h in range(N_Q_HEADS)], dtype=np.float32)
    bias = -slopes[:, None, None] * dist[None]
    general = np.where(in_band[None], bias, -np.inf)
    first = np.where((in_band & (j >= CHUNK))[None], bias, -np.inf)
    return np.stack([general, first]).astype(np.float32)


def _mixer_b_kernel(sinks_ref, h_ref, kvc_ref, kvp_ref, g_ref, wq_ref, wo_ref, bias_ref, o_ref,
                    q_ref, att_ref):
    tm = h_ref.shape[0]
    blocks_per_seq = SEQ // CHUNK
    blocks_per_tile = tm // CHUNK
    kv_width = N_KV_HEADS * HEAD_DIM

    h = h_ref[...]
    q = _dot(_rms(h, g_ref[...]).astype(_BF16), wq_ref[...]) * (HEAD_DIM ** -0.5)
    q_ref[...] = q.astype(_BF16)

    lane = lax.broadcasted_iota(jnp.int32, (2 * CHUNK, LANES), 1)
    low_half = lane < HEAD_DIM
    for b in range(blocks_per_tile):
        rows = slice(b * CHUNK, (b + 1) * CHUNK)
        block = pl.program_id(0) * blocks_per_tile + b
        first = jnp.where(block % blocks_per_seq == 0, 1, 0)
        if b == 0:
            kv_prev = kvp_ref[...]
        else:
            kv_prev = kvc_ref[(b - 1) * CHUNK:b * CHUNK, :]
        kv = jnp.concatenate([kv_prev, kvc_ref[rows, :]], axis=0)
        for pair in range(N_KV_HEADS // 2):
            k_pair = kv[:, pair * LANES:(pair + 1) * LANES]
            v_pair = kv[:, kv_width + pair * LANES:kv_width + (pair + 1) * LANES]
            groups = [pair * GQA_GROUP + g for g in range(GQA_GROUP)]
            q_stack = jnp.concatenate([q_ref[rows, j * LANES:(j + 1) * LANES] for j in groups], axis=0)
            probs = []
            for half in range(2):
                kv_head = 2 * pair + half
                k_half = jnp.where(low_half if half == 0 else ~low_half, k_pair, 0)
                s = lax.dot_general(q_stack, k_half, (((1,), (1,)), ((), ())),
                                    preferred_element_type=_F32)
                parts = []
                for g in range(GQA_GROUP):
                    head = kv_head * GQA_GROUP + g
                    sg = s[g * CHUNK:(g + 1) * CHUNK, :] + bias_ref[first, head]
                    sink = sinks_ref[head]
                    mx = jnp.maximum(jnp.max(sg, axis=-1, keepdims=True), sink)
                    e = jnp.exp(sg - mx)
                    denom = jnp.sum(e, axis=-1, keepdims=True) + jnp.exp(sink - mx)
                    parts.append((e * (1.0 / denom)).astype(_BF16))
                probs.append(jnp.concatenate(parts, axis=0))
            p_cat = jnp.concatenate(probs, axis=1)
            v_cat = jnp.concatenate([jnp.where(low_half, v_pair, 0), jnp.where(low_half, 0, v_pair)], axis=0)
            out = _dot(p_cat, v_cat)
            for g in range(GQA_GROUP):
                j = groups[g]
                att_ref[rows, j * LANES:(j + 1) * LANES] = out[g * CHUNK:(g + 1) * CHUNK, :].astype(_BF16)
    o_ref[...] = h + _dot(att_ref[...], wo_ref[...])


def _mixer_b(h, kv, g, wq, wo, sinks, bias):
    m = h.shape[0]
    tm = TM_B
    blocks_per_tile = tm // CHUNK
    kvw = kv.shape[1]
    resident = [_nbytes(wq.shape, _BF16), _nbytes(wo.shape, _BF16), _nbytes(bias.shape, _F32)]
    streamed = [2 * _nbytes((tm, D_MODEL), _F32), _nbytes((tm + CHUNK, kvw), _BF16)]
    scratch = [2 * _nbytes((tm, D_MODEL), _BF16)]
    temporaries = [3 * _nbytes((tm, D_MODEL), _F32), 8 * _nbytes((GQA_GROUP * CHUNK, 2 * CHUNK), _F32)]

    return pl.pallas_call(
        _mixer_b_kernel,
        out_shape=jax.ShapeDtypeStruct((m, D_MODEL), _F32),
        grid=(m // tm,),
        in_specs=[
            pl.BlockSpec(memory_space=pltpu.SMEM),
            _row_spec(tm, D_MODEL),
            _row_spec(tm, kvw),
            pl.BlockSpec((CHUNK, kvw), lambda i: (jnp.maximum(i * blocks_per_tile - 1, 0), 0)),
            _const_spec(g.shape),
            _const_spec(wq.shape),
            _const_spec(wo.shape),
            _const_spec(bias.shape),
        ],
        out_specs=_row_spec(tm, D_MODEL),
        scratch_shapes=[pltpu.VMEM((tm, D_MODEL), _BF16), pltpu.VMEM((tm, D_MODEL), _BF16)],
        compiler_params=pltpu.CompilerParams(
            dimension_semantics=("arbitrary",),
            vmem_limit_bytes=_vmem_limit(resident, streamed, scratch, temporaries)),
        name="mixer_b",
    )(sinks, h, kv, kv, g, wq, wo, bias)


def kernel(x, p, norm_mix, norm_ffn, norm_ple, norm_kv, norm_final, a_w_in, a_norm_v, a_w_s, a_b_s, a_w_out, w_kv, b_w_q, b_sinks, b_w_o, f_w_up, f_conv_w, f_conv_b, f_w_down, ple_w_in, ple_w_gate, ple_b_gate):
    batch, seq, d = x.shape
    assert (seq, d) == (SEQ, D_MODEL) and seq % max(TM_A, TM_F, TM_B) == 0
    m = batch * seq
    row = lambda a: a.reshape(1, -1)

    h = x.reshape(m, d)
    bias_s = jnp.repeat(a_b_s[0].T, LANES, axis=1)
    h = _mixer_a(h, row(norm_mix[0]), a_w_in[0].astype(_BF16), row(a_norm_v[0]), a_w_s[0], bias_s,
                 a_w_out[0].astype(_BF16))

    rows = lambda a: a[:, None, :]
    stacked = (rows(norm_ffn), f_w_up.astype(_BF16), f_conv_w, rows(f_conv_b), f_w_down.astype(_BF16),
               rows(norm_ple), ple_w_in.astype(_BF16), ple_w_gate.astype(_BF16), rows(ple_b_gate))
    p = p.reshape(p.shape[0], m, PLE_DIM)

    h, kv = _ffn_ple(h, p, 0, stacked, row(norm_kv), w_kv.astype(_BF16), name="ffn_ple_0")

    h = _mixer_b(h, kv, row(norm_mix[1]), _pair_heads(b_w_q[0], 1).astype(_BF16),
                 _pair_heads(b_w_o[0], 0).astype(_BF16), b_sinks[0], jnp.asarray(_attention_bias()))

    out = _ffn_ple(h, p, 1, stacked, row(norm_final), name="ffn_ple_1")
    return out.reshape(batch, seq, d)
```

```python
import functools
import math

import jax
import jax.numpy as jnp
import numpy as np
from jax import lax
from jax.experimental import pallas as pl
from jax.experimental.pallas import tpu as pltpu

D_MODEL = 1024
SEQ = 8192
CHUNK = 128
A_GROUPS = 8
HEAD_DIM = 64
N_Q_HEADS = 16
N_KV_HEADS = 4
GQA_GROUP = N_Q_HEADS // N_KV_HEADS
D_FF = 2816
CONV_WIDTH = 3
PLE_DIM = 256
EPS = 1e-6

LANES = 128
SUBLANES = 8
MXU_WIDTH = 256
V7X_VMEM_BYTES = 64 * 1024 * 1024

TM_A = 1024
TM_F = 1024
TM_B = 1024
SOFTMAX_ROWS = 32
FF_CHUNK = MXU_WIDTH
N_FF_CHUNKS = D_FF // FF_CHUNK
GROUP_TILES = 8
GROUP = SUBLANES * GROUP_TILES
CARRY_ROWS = (CONV_WIDTH - 1) * SUBLANES

_BF16 = jnp.bfloat16
_F32 = jnp.float32


def _dot(a, b):
    return jnp.dot(a, b, preferred_element_type=_F32)


def _rms(x, g):
    return x * lax.rsqrt(jnp.mean(x * x, axis=-1, keepdims=True) + EPS) * g


def _silu(x):
    half = 0.5 * x
    return half * (jnp.tanh(half) + 1.0)


def _const_spec(shape):
    zeros = (0,) * len(shape)
    return pl.BlockSpec(shape, lambda i: zeros, pipeline_mode=pl.Buffered(1))


def _row_spec(tm, width):
    return pl.BlockSpec((tm, width), lambda i: (i, 0))


def _nbytes(shape, dtype):
    return math.prod(shape) * jnp.dtype(dtype).itemsize


def _vmem_limit(resident, streamed, scratch, temporaries):
    total = sum(resident) + 2 * sum(streamed) + sum(scratch) + sum(temporaries)
    return min(int(total), V7X_VMEM_BYTES)


def _mixer_a_kernel(x_ref, g_ref, w_in_ref, gv_ref, ws_ref, bs_ref, w_out_ref, o_ref,
                    z_ref, v_ref, gated_ref):
    tm = x_ref.shape[0]
    xn = _rms(x_ref[...], g_ref[...]).astype(_BF16)
    z = _dot(xn, w_in_ref[...])
    z_ref[...] = jax.nn.gelu(z, approximate=True)
    v_ref[...] = _rms(z_ref[:, D_MODEL:], gv_ref[...]).astype(_BF16)

    row = lax.broadcasted_iota(jnp.int32, (CHUNK, CHUNK), 0)
    col = lax.broadcasted_iota(jnp.int32, (CHUNK, CHUNK), 1)
    causal = row >= col
    for h in range(A_GROUPS):
        w_h = jnp.where(causal, ws_ref[h], 0.0).astype(_BF16)
        lanes = slice(h * LANES, (h + 1) * LANES)
        for c in range(tm // CHUNK):
            rows = slice(c * CHUNK, (c + 1) * CHUNK)
            s = _dot(w_h, v_ref[rows, lanes]) + bs_ref[:, lanes]
            gated_ref[rows, lanes] = (z_ref[rows, lanes] * s).astype(_BF16)
    o_ref[...] = x_ref[...] + _dot(gated_ref[...], w_out_ref[...])


def _mixer_a(x, g, w_in, gv, ws, bs, w_out):
    m = x.shape[0]
    tm = TM_A
    resident = [_nbytes(w_in.shape, _BF16), _nbytes(w_out.shape, _BF16), _nbytes(ws.shape, _F32),
                _nbytes(bs.shape, _F32)]
    streamed = [2 * _nbytes((tm, D_MODEL), _F32)]
    scratch = [_nbytes((tm, 2 * D_MODEL), _F32), 2 * _nbytes((tm, D_MODEL), _BF16)]
    temporaries = [2 * _nbytes((tm, 2 * D_MODEL), _F32)]
    return pl.pallas_call(
        _mixer_a_kernel,
        out_shape=jax.ShapeDtypeStruct((m, D_MODEL), _F32),
        grid=(m // tm,),
        in_specs=[
            _row_spec(tm, D_MODEL),
            _const_spec(g.shape),
            _const_spec(w_in.shape),
            _const_spec(gv.shape),
            _const_spec(ws.shape),
            _const_spec(bs.shape),
            _const_spec(w_out.shape),
        ],
        out_specs=_row_spec(tm, D_MODEL),
        scratch_shapes=[
            pltpu.VMEM((tm, 2 * D_MODEL), _F32),
            pltpu.VMEM((tm, D_MODEL), _BF16),
            pltpu.VMEM((tm, D_MODEL), _BF16),
        ],
        compiler_params=pltpu.CompilerParams(
            dimension_semantics=("arbitrary",),
            vmem_limit_bytes=_vmem_limit(resident, streamed, scratch, temporaries)),
        name="mixer_a",
    )(x, g, w_in, gv, ws, bs, w_out)


def _ffn_ple_kernel(final_norm, h_ref, p_ref, gf_ref, wup_ref, cw_ref, cb_ref, wdown_ref,
                    gp_ref, wpi_ref, wg_ref, bg_ref, ge_ref, *rest):
    if final_norm:
        o_ref, slab_ref, hn_ref, carry_ref, act_ref = rest
    else:
        we_ref, o_ref, kv_ref, slab_ref, hn_ref, carry_ref, act_ref = rest
    tm = h_ref.shape[0]
    first_of_seq = (pl.program_id(0) % (SEQ // tm)) == 0
    n_groups = tm // GROUP
    n_slabs = D_MODEL // LANES
    chunk_tiles = FF_CHUNK // LANES
    lanes = lambda l: slice(l * LANES, (l + 1) * LANES)

    hn = _rms(h_ref[...], gf_ref[...])
    for j in range(n_slabs):
        slab_ref[j] = hn[:, lanes(j)]
    for j in range(n_slabs):
        for k in range(n_groups):
            for q in range(0, GROUP_TILES, 2):
                pair = [slab_ref[j, pl.ds(GROUP * k + q + d, SUBLANES, stride=GROUP_TILES), :] for d in range(2)]
                rows = slice(GROUP * k + SUBLANES * q, GROUP * k + SUBLANES * (q + 2))
                hn_ref[rows, lanes(j)] = jnp.concatenate(pair, axis=0).astype(_BF16)

    first_row = lax.broadcasted_iota(jnp.int32, (SUBLANES, LANES), 0) == 0
    tails = {}

    def conv_block(c, half, j, k):
        col = slab_ref.at[2 * chunk_tiles * (c % 2) + chunk_tiles * half + j]
        blk = col[GROUP * k:GROUP * (k + 1), :]
        prev = tails[c, half][:, lanes(j)] if k == 0 else col[GROUP * k - CARRY_ROWS:GROUP * k, :]
        start = half * D_FF + c * FF_CHUNK + j * LANES
        wl = slice(start, start + LANES)
        edge2, edge1 = (
            jnp.where(first_row, pltpu.roll(prev[s:s + SUBLANES], 1, 0),
                      pltpu.roll(blk[GROUP - 2 * SUBLANES + s:GROUP - SUBLANES + s], 1, 0))
            for s in (0, SUBLANES))
        back1 = jnp.concatenate([edge1, blk[:GROUP - SUBLANES]], axis=0)
        back2 = jnp.concatenate([edge2, edge1, blk[:GROUP - 2 * SUBLANES]], axis=0)
        w0, w1, w2 = (cw_ref[t:t + 1, wl] for t in range(CONV_WIDTH))
        return cb_ref[:, wl] + back2 * w0 + back1 * w1 + blk * w2

    for c in range(N_FF_CHUNKS + 1):
        if c < N_FF_CHUNKS:
            for half in range(2):
                start = half * D_FF + c * FF_CHUNK
                up = _dot(hn_ref[...], wup_ref[:, start:start + FF_CHUNK])
                tails[c, half] = jnp.where(first_of_seq, 0.0, carry_ref[half, c])
                carry_ref[half, c] = up[tm - CARRY_ROWS:, :]
                for j in range(chunk_tiles):
                    slab_ref[2 * chunk_tiles * (c % 2) + chunk_tiles * half + j] = up[:, lanes(j)]
        if c >= 1:
            for j in range(chunk_tiles):
                for k in range(n_groups):
                    act_ref[GROUP * k:GROUP * (k + 1), lanes((c - 1) * chunk_tiles + j)] = (
                        _silu(conv_block(c - 1, 0, j, k)) * conv_block(c - 1, 1, j, k)).astype(_BF16)

    down = _dot(act_ref[...], wdown_ref[...])
    for j in range(n_slabs):
        for k in range(n_groups):
            for q in range(GROUP_TILES):
                rows = slice(GROUP * k + SUBLANES * q, GROUP * k + SUBLANES * (q + 1))
                slab_ref[j, pl.ds(GROUP * k + q, SUBLANES, stride=GROUP_TILES), :] = down[rows, lanes(j)]
    h = h_ref[...] + jnp.concatenate([slab_ref[j] for j in range(n_slabs)], axis=1)

    hn = _rms(h, gp_ref[...]).astype(_BF16)
    gate = jax.nn.sigmoid(_dot(hn, wg_ref[...]) + bg_ref[...])
    h = h + _dot(p_ref[...].astype(_BF16), wpi_ref[...]) * gate

    if final_norm:
        o_ref[...] = _rms(h, ge_ref[...])
    else:
        o_ref[...] = h
        kv_ref[...] = _dot(_rms(h, ge_ref[...]).astype(_BF16), we_ref[...]).astype(_BF16)


def _layer_spec(shape, layer):
    zeros = (0,) * (len(shape) - 1)
    return pl.BlockSpec((None,) + tuple(shape[1:]), lambda i: (layer,) + zeros, pipeline_mode=pl.Buffered(1))


def _ffn_ple(h, p, layer, stacked, ge, we=None, *, name):
    m = h.shape[0]
    tm = TM_F
    final_norm = we is None
    consts = (ge,) + (() if final_norm else (we,))
    out_shape = [jax.ShapeDtypeStruct((m, D_MODEL), _F32)]
    out_specs = [_row_spec(tm, D_MODEL)]
    streamed = [2 * _nbytes((tm, D_MODEL), _F32), _nbytes((tm, PLE_DIM), _F32)]
    if not final_norm:
        out_shape.append(jax.ShapeDtypeStruct((m, we.shape[1]), _BF16))
        out_specs.append(_row_spec(tm, we.shape[1]))
        streamed.append(_nbytes((tm, we.shape[1]), _BF16))
    scratch_shapes = [
        pltpu.VMEM((D_MODEL // LANES, tm, LANES), _F32),
        pltpu.VMEM((tm, D_MODEL), _BF16),
        pltpu.VMEM((2, N_FF_CHUNKS, CARRY_ROWS, FF_CHUNK), _F32),
        pltpu.VMEM((tm, D_FF), _BF16),
    ]
    resident = [_nbytes(w.shape[1:], w.dtype) for w in stacked] + [_nbytes(w.shape, w.dtype) for w in consts]
    scratch = [_nbytes(s.shape, s.dtype) for s in scratch_shapes]
    temporaries = [4 * _nbytes((tm, D_MODEL), _F32), 2 * _nbytes((tm, 2 * FF_CHUNK), _F32)]
    outs = pl.pallas_call(
        functools.partial(_ffn_ple_kernel, final_norm),
        out_shape=out_shape,
        grid=(m // tm,),
        in_specs=([_row_spec(tm, D_MODEL), pl.BlockSpec((None, tm, PLE_DIM), lambda i: (layer, i, 0))]
                  + [_layer_spec(w.shape, layer) for w in stacked] + [_const_spec(w.shape) for w in consts]),
        out_specs=out_specs,
        scratch_shapes=scratch_shapes,
        compiler_params=pltpu.CompilerParams(
            dimension_semantics=("arbitrary",),
            vmem_limit_bytes=_vmem_limit(resident, streamed, scratch, temporaries)),
        name=name,
    )(h, p, *stacked, *consts)
    return outs[0] if final_norm else outs


def _pair_heads(w, axis):
    shape = w.shape
    split = shape[:axis] + (N_KV_HEADS // 2, 2, GQA_GROUP, HEAD_DIM) + shape[axis + 1:]
    return jnp.swapaxes(w.reshape(split), axis + 1, axis + 2).reshape(shape)


def _attention_bias():
    i = np.arange(CHUNK)[:, None]
    j = np.arange(2 * CHUNK)[None, :]
    dist = (i + CHUNK - j).astype(np.float32)
    in_band = (dist >= 0) & (dist < CHUNK)
    slopes = np.array([2.0 ** (-8.0 * (h + 1) / N_Q_HEADS) for h in range(N_Q_HEADS)], dtype=np.float32)
    bias = -slopes[:, None, None] * dist[None]
    general = np.where(in_band[None], bias, -np.inf)
    first = np.where((in_band & (j >= CHUNK))[None], bias, -np.inf)
    return np.stack([general, first]).astype(np.float32)


def _mixer_b_kernel(sinks_ref, h_ref, kvc_ref, kvp_ref, g_ref, wq_ref, wo_ref, bias_ref, o_ref,
                    q_ref, att_ref):
    tm = h_ref.shape[0]
    blocks_per_seq = SEQ // CHUNK
    blocks_per_tile = tm // CHUNK
    kv_width = N_KV_HEADS * HEAD_DIM

    q = _dot(_rms(h_ref[...], g_ref[...]).astype(_BF16), wq_ref[...]) * (HEAD_DIM ** -0.5)
    q_ref[...] = q.astype(_BF16)

    lane = lax.broadcasted_iota(jnp.int32, (2 * CHUNK, LANES), 1)
    low_half = lane < HEAD_DIM
    low_half_q = lax.broadcasted_iota(jnp.int32, (SOFTMAX_ROWS, LANES), 1) < HEAD_DIM
    for b in range(blocks_per_tile):
        rows = slice(b * CHUNK, (b + 1) * CHUNK)
        block = pl.program_id(0) * blocks_per_tile + b
        first = jnp.where(block % blocks_per_seq == 0, 1, 0)
        if b == 0:
            kv_prev = kvp_ref[...]
        else:
            kv_prev = kvc_ref[(b - 1) * CHUNK:b * CHUNK, :]
        kv = jnp.concatenate([kv_prev, kvc_ref[rows, :]], axis=0)
        for pair in range(N_KV_HEADS // 2):
            k_pair = kv[:, pair * LANES:(pair + 1) * LANES]
            v_pair = kv[:, kv_width + pair * LANES:kv_width + (pair + 1) * LANES]
            groups = [pair * GQA_GROUP + g for g in range(GQA_GROUP)]
            q_stack = jnp.concatenate([q_ref[rows, j * LANES:(j + 1) * LANES] for j in groups], axis=0)
            weights, recips = [], []
            for half in range(2):
                kv_head = 2 * pair + half
                k_half = jnp.where(low_half if half == 0 else ~low_half, k_pair, 0)
                s = lax.dot_general(q_stack, k_half, (((1,), (1,)), ((), ())),
                                    preferred_element_type=_F32)
                parts, rparts = [], []
                for g in range(GQA_GROUP):
                    head = kv_head * GQA_GROUP + g
                    sink = sinks_ref[head]
                    for r0 in range(0, CHUNK, SOFTMAX_ROWS):
                        sg = (s[g * CHUNK + r0:g * CHUNK + r0 + SOFTMAX_ROWS, :]
                              + bias_ref[first, head, r0:r0 + SOFTMAX_ROWS, :])
                        mx = jnp.maximum(jnp.max(sg, axis=-1, keepdims=True), sink)
                        e = jnp.exp(sg - mx)
                        denom = jnp.sum(e, axis=-1, keepdims=True) + jnp.exp(sink - mx)
                        parts.append(e.astype(_BF16))
                        rparts.append(1.0 / denom)
                weights.append(jnp.concatenate(parts, axis=0))
                recips.append(rparts)
            p_cat = jnp.concatenate(weights, axis=1)
            v_cat = jnp.concatenate([jnp.where(low_half, v_pair, 0), jnp.where(low_half, 0, v_pair)], axis=0)
            out = _dot(p_cat, v_cat)
            for g in range(GQA_GROUP):
                j = groups[g]
                for t, r0 in enumerate(range(0, CHUNK, SOFTMAX_ROWS)):
                    out_t = out[g * CHUNK + r0:g * CHUNK + r0 + SOFTMAX_ROWS, :]
                    n = g * (CHUNK // SOFTMAX_ROWS) + t
                    att_ref[b * CHUNK + r0:b * CHUNK + r0 + SOFTMAX_ROWS, j * LANES:(j + 1) * LANES] = jnp.where(
                        low_half_q, out_t * recips[0][n], out_t * recips[1][n]).astype(_BF16)
    o_ref[...] = h_ref[...] + _dot(att_ref[...], wo_ref[...])


def _mixer_b(h, kv, g, wq, wo, sinks, bias):
    m = h.shape[0]
    tm = TM_B
    blocks_per_tile = tm // CHUNK
    kvw = kv.shape[1]
    resident = [_nbytes(wq.shape, _BF16), _nbytes(wo.shape, _BF16), _nbytes(bias.shape, _F32)]
    streamed = [2 * _nbytes((tm, D_MODEL), _F32), _nbytes((tm + CHUNK, kvw), _BF16)]
    scratch = [2 * _nbytes((tm, D_MODEL), _BF16)]
    temporaries = [3 * _nbytes((tm, D_MODEL), _F32), 8 * _nbytes((GQA_GROUP * CHUNK, 2 * CHUNK), _F32)]

    return pl.pallas_call(
        _mixer_b_kernel,
        out_shape=jax.ShapeDtypeStruct((m, D_MODEL), _F32),
        grid=(m // tm,),
        in_specs=[
            pl.BlockSpec(memory_space=pltpu.SMEM),
            _row_spec(tm, D_MODEL),
            _row_spec(tm, kvw),
            pl.BlockSpec((CHUNK, kvw), lambda i: (jnp.maximum(i * blocks_per_tile - 1, 0), 0)),
            _const_spec(g.shape),
            _const_spec(wq.shape),
            _const_spec(wo.shape),
            _const_spec(bias.shape),
        ],
        out_specs=_row_spec(tm, D_MODEL),
        scratch_shapes=[pltpu.VMEM((tm, D_MODEL), _BF16), pltpu.VMEM((tm, D_MODEL), _BF16)],
        compiler_params=pltpu.CompilerParams(
            dimension_semantics=("arbitrary",),
            vmem_limit_bytes=_vmem_limit(resident, streamed, scratch, temporaries)),
        name="mixer_b",
    )(sinks, h, kv, kv, g, wq, wo, bias)


def kernel(x, p, norm_mix, norm_ffn, norm_ple, norm_kv, norm_final, a_w_in, a_norm_v, a_w_s, a_b_s, a_w_out, w_kv, b_w_q, b_sinks, b_w_o, f_w_up, f_conv_w, f_conv_b, f_w_down, ple_w_in, ple_w_gate, ple_b_gate):
    batch, seq, d = x.shape
    assert (seq, d) == (SEQ, D_MODEL) and seq % max(TM_A, TM_F, TM_B) == 0
    m = batch * seq
    row = lambda a: a.reshape(1, -1)

    h = x.reshape(m, d)
    bias_s = jnp.repeat(a_b_s[0].T, LANES, axis=1)
    h = _mixer_a(h, row(norm_mix[0]), a_w_in[0].astype(_BF16), row(a_norm_v[0]), a_w_s[0], bias_s,
                 a_w_out[0].astype(_BF16))

    rows = lambda a: a[:, None, :]
    stacked = (rows(norm_ffn), f_w_up.astype(_BF16), f_conv_w, rows(f_conv_b), f_w_down.astype(_BF16),
               rows(norm_ple), ple_w_in.astype(_BF16), ple_w_gate.astype(_BF16), rows(ple_b_gate))
    p = p.reshape(p.shape[0], m, PLE_DIM)

    h, kv = _ffn_ple(h, p, 0, stacked, row(norm_kv), w_kv.astype(_BF16), name="ffn_ple_0")

    h = _mixer_b(h, kv, row(norm_mix[1]), _pair_heads(b_w_q[0], 1).astype(_BF16),
                 _pair_heads(b_w_o[0], 0).astype(_BF16), b_sinks[0], jnp.asarray(_attention_bias()))

    out = _ffn_ple(h, p, 1, stacked, row(norm_final), name="ffn_ple_1")
    return out.reshape(batch, seq, d)
```

```python
import functools
import math

import jax
import jax.numpy as jnp
import numpy as np
from jax import lax
from jax.experimental import pallas as pl
from jax.experimental.pallas import tpu as pltpu

D_MODEL = 1024
SEQ = 8192
CHUNK = 128
A_GROUPS = 8
HEAD_DIM = 64
N_Q_HEADS = 16
N_KV_HEADS = 4
GQA_GROUP = N_Q_HEADS // N_KV_HEADS
D_FF = 2816
CONV_WIDTH = 3
PLE_DIM = 256
EPS = 1e-6

LANES = 128
SUBLANES = 8
MXU_WIDTH = 256
V7X_VMEM_BYTES = 64 * 1024 * 1024

TM_A = 1024
TM_F = 1024
TM_B = 1024
FF_CHUNK = MXU_WIDTH
N_FF_CHUNKS = D_FF // FF_CHUNK
GROUP_TILES = 8
GROUP = SUBLANES * GROUP_TILES
CARRY_ROWS = (CONV_WIDTH - 1) * SUBLANES

_BF16 = jnp.bfloat16
_F32 = jnp.float32


def _dot(a, b):
    return jnp.dot(a, b, preferred_element_type=_F32)


def _rms(x, g):
    return x * lax.rsqrt(jnp.mean(x * x, axis=-1, keepdims=True) + EPS) * g


def _silu(x):
    half = 0.5 * x
    return half * (jnp.tanh(half) + 1.0)


def _const_spec(shape):
    zeros = (0,) * len(shape)
    return pl.BlockSpec(shape, lambda i: zeros, pipeline_mode=pl.Buffered(1))


def _row_spec(tm, width):
    return pl.BlockSpec((tm, width), lambda i: (i, 0))


def _nbytes(shape, dtype):
    return math.prod(shape) * jnp.dtype(dtype).itemsize


def _vmem_limit(resident, streamed, scratch, temporaries):
    total = sum(resident) + 2 * sum(streamed) + sum(scratch) + sum(temporaries)
    return min(int(total), V7X_VMEM_BYTES)


def _mixer_a_kernel(x_ref, g_ref, w_in_ref, gv_ref, ws_ref, bs_ref, w_out_ref, o_ref,
                    z_ref, v_ref, gated_ref):
    tm = x_ref.shape[0]
    xn = _rms(x_ref[...], g_ref[...]).astype(_BF16)
    z = _dot(xn, w_in_ref[...])
    z_ref[...] = jax.nn.gelu(z, approximate=True)
    v_ref[...] = _rms(z_ref[:, D_MODEL:], gv_ref[...]).astype(_BF16)

    row = lax.broadcasted_iota(jnp.int32, (CHUNK, CHUNK), 0)
    col = lax.broadcasted_iota(jnp.int32, (CHUNK, CHUNK), 1)
    causal = row >= col
    for h in range(A_GROUPS):
        w_h = jnp.where(causal, ws_ref[h], 0.0).astype(_BF16)
        lanes = slice(h * LANES, (h + 1) * LANES)
        for c in range(tm // CHUNK):
            rows = slice(c * CHUNK, (c + 1) * CHUNK)
            s = _dot(w_h, v_ref[rows, lanes]) + bs_ref[:, lanes]
            gated_ref[rows, lanes] = (z_ref[rows, lanes] * s).astype(_BF16)
    o_ref[...] = x_ref[...] + _dot(gated_ref[...], w_out_ref[...])


def _mixer_a(x, g, w_in, gv, ws, bs, w_out):
    m = x.shape[0]
    tm = TM_A
    resident = [_nbytes(w_in.shape, _BF16), _nbytes(w_out.shape, _BF16), _nbytes(ws.shape, _F32),
                _nbytes(bs.shape, _F32)]
    streamed = [2 * _nbytes((tm, D_MODEL), _F32)]
    scratch = [_nbytes((tm, 2 * D_MODEL), _F32), 2 * _nbytes((tm, D_MODEL), _BF16)]
    temporaries = [2 * _nbytes((tm, 2 * D_MODEL), _F32)]
    return pl.pallas_call(
        _mixer_a_kernel,
        out_shape=jax.ShapeDtypeStruct((m, D_MODEL), _F32),
        grid=(m // tm,),
        in_specs=[
            _row_spec(tm, D_MODEL),
            _const_spec(g.shape),
            _const_spec(w_in.shape),
            _const_spec(gv.shape),
            _const_spec(ws.shape),
            _const_spec(bs.shape),
            _const_spec(w_out.shape),
        ],
        out_specs=_row_spec(tm, D_MODEL),
        scratch_shapes=[
            pltpu.VMEM((tm, 2 * D_MODEL), _F32),
            pltpu.VMEM((tm, D_MODEL), _BF16),
            pltpu.VMEM((tm, D_MODEL), _BF16),
        ],
        compiler_params=pltpu.CompilerParams(
            dimension_semantics=("arbitrary",),
            vmem_limit_bytes=_vmem_limit(resident, streamed, scratch, temporaries)),
        name="mixer_a",
    )(x, g, w_in, gv, ws, bs, w_out)


def _ffn_ple_kernel(final_norm, h_ref, p_ref, gf_ref, wup_ref, cw_ref, cb_ref, wdown_ref,
                    gp_ref, wpi_ref, wg_ref, bg_ref, ge_ref, *rest):
    if final_norm:
        o_ref, slab_ref, hn_ref, carry_ref, act_ref = rest
    else:
        we_ref, o_ref, kv_ref, slab_ref, hn_ref, carry_ref, act_ref = rest
    tm = h_ref.shape[0]
    first_of_seq = (pl.program_id(0) % (SEQ // tm)) == 0
    n_groups = tm // GROUP
    n_slabs = D_MODEL // LANES
    chunk_tiles = FF_CHUNK // LANES
    lanes = lambda l: slice(l * LANES, (l + 1) * LANES)

    hn = _rms(h_ref[...], gf_ref[...])
    for j in range(n_slabs):
        slab_ref[j] = hn[:, lanes(j)]
    for j in range(n_slabs):
        for k in range(n_groups):
            for q in range(0, GROUP_TILES, 2):
                pair = [slab_ref[j, pl.ds(GROUP * k + q + d, SUBLANES, stride=GROUP_TILES), :] for d in range(2)]
                rows = slice(GROUP * k + SUBLANES * q, GROUP * k + SUBLANES * (q + 2))
                hn_ref[rows, lanes(j)] = jnp.concatenate(pair, axis=0).astype(_BF16)

    first_row = lax.broadcasted_iota(jnp.int32, (SUBLANES, LANES), 0) == 0
    tails = {}

    def conv_block(c, half, j, k):
        col = slab_ref.at[2 * chunk_tiles * (c % 2) + chunk_tiles * half + j]
        blk = col[GROUP * k:GROUP * (k + 1), :]
        prev = tails[c, half][:, lanes(j)] if k == 0 else col[GROUP * k - CARRY_ROWS:GROUP * k, :]
        start = half * D_FF + c * FF_CHUNK + j * LANES
        wl = slice(start, start + LANES)
        edge2, edge1 = (
            jnp.where(first_row, pltpu.roll(prev[s:s + SUBLANES], 1, 0),
                      pltpu.roll(blk[GROUP - 2 * SUBLANES + s:GROUP - SUBLANES + s], 1, 0))
            for s in (0, SUBLANES))
        back1 = jnp.concatenate([edge1, blk[:GROUP - SUBLANES]], axis=0)
        back2 = jnp.concatenate([edge2, edge1, blk[:GROUP - 2 * SUBLANES]], axis=0)
        w0, w1, w2 = (cw_ref[t:t + 1, wl] for t in range(CONV_WIDTH))
        return cb_ref[:, wl] + back2 * w0 + back1 * w1 + blk * w2

    for c in range(N_FF_CHUNKS + 1):
        if c < N_FF_CHUNKS:
            for half in range(2):
                start = half * D_FF + c * FF_CHUNK
                up = _dot(hn_ref[...], wup_ref[:, start:start + FF_CHUNK])
                tails[c, half] = jnp.where(first_of_seq, 0.0, carry_ref[half, c])
                carry_ref[half, c] = up[tm - CARRY_ROWS:, :]
                for j in range(chunk_tiles):
                    slab_ref[2 * chunk_tiles * (c % 2) + chunk_tiles * half + j] = up[:, lanes(j)]
        if c >= 1:
            for j in range(chunk_tiles):
                for k in range(n_groups):
                    act_ref[GROUP * k:GROUP * (k + 1), lanes((c - 1) * chunk_tiles + j)] = (
                        _silu(conv_block(c - 1, 0, j, k)) * conv_block(c - 1, 1, j, k)).astype(_BF16)

    down = _dot(act_ref[...], wdown_ref[...])
    for j in range(n_slabs):
        for k in range(n_groups):
            for q in range(GROUP_TILES):
                rows = slice(GROUP * k + SUBLANES * q, GROUP * k + SUBLANES * (q + 1))
                slab_ref[j, pl.ds(GROUP * k + q, SUBLANES, stride=GROUP_TILES), :] = down[rows, lanes(j)]
    h = h_ref[...] + jnp.concatenate([slab_ref[j] for j in range(n_slabs)], axis=1)

    hn = _rms(h, gp_ref[...]).astype(_BF16)
    gate = jax.nn.sigmoid(_dot(hn, wg_ref[...]) + bg_ref[...])
    h = h + _dot(p_ref[...].astype(_BF16), wpi_ref[...]) * gate

    if final_norm:
        o_ref[...] = _rms(h, ge_ref[...])
    else:
        o_ref[...] = h
        kv_ref[...] = _dot(_rms(h, ge_ref[...]).astype(_BF16), we_ref[...]).astype(_BF16)


def _layer_spec(shape, layer):
    zeros = (0,) * (len(shape) - 1)
    return pl.BlockSpec((None,) + tuple(shape[1:]), lambda i: (layer,) + zeros, pipeline_mode=pl.Buffered(1))


def _ffn_ple(h, p, layer, stacked, ge, we=None, *, name):
    m = h.shape[0]
    tm = TM_F
    final_norm = we is None
    consts = (ge,) + (() if final_norm else (we,))
    out_shape = [jax.ShapeDtypeStruct((m, D_MODEL), _F32)]
    out_specs = [_row_spec(tm, D_MODEL)]
    streamed = [2 * _nbytes((tm, D_MODEL), _F32), _nbytes((tm, PLE_DIM), _F32)]
    if not final_norm:
        out_shape.append(jax.ShapeDtypeStruct((m, we.shape[1]), _BF16))
        out_specs.append(_row_spec(tm, we.shape[1]))
        streamed.append(_nbytes((tm, we.shape[1]), _BF16))
    scratch_shapes = [
        pltpu.VMEM((D_MODEL // LANES, tm, LANES), _F32),
        pltpu.VMEM((tm, D_MODEL), _BF16),
        pltpu.VMEM((2, N_FF_CHUNKS, CARRY_ROWS, FF_CHUNK), _F32),
        pltpu.VMEM((tm, D_FF), _BF16),
    ]
    resident = [_nbytes(w.shape[1:], w.dtype) for w in stacked] + [_nbytes(w.shape, w.dtype) for w in consts]
    scratch = [_nbytes(s.shape, s.dtype) for s in scratch_shapes]
    temporaries = [4 * _nbytes((tm, D_MODEL), _F32), 2 * _nbytes((tm, 2 * FF_CHUNK), _F32)]
    outs = pl.pallas_call(
        functools.partial(_ffn_ple_kernel, final_norm),
        out_shape=out_shape,
        grid=(m // tm,),
        in_specs=([_row_spec(tm, D_MODEL), pl.BlockSpec((None, tm, PLE_DIM), lambda i: (layer, i, 0))]
                  + [_layer_spec(w.shape, layer) for w in stacked] + [_const_spec(w.shape) for w in consts]),
        out_specs=out_specs,
        scratch_shapes=scratch_shapes,
        compiler_params=pltpu.CompilerParams(
            dimension_semantics=("arbitrary",),
            vmem_limit_bytes=_vmem_limit(resident, streamed, scratch, temporaries)),
        name=name,
    )(h, p, *stacked, *consts)
    return outs[0] if final_norm else outs


def _pair_heads(w, axis):
    shape = w.shape
    split = shape[:axis] + (N_KV_HEADS // 2, 2, GQA_GROUP, HEAD_DIM) + shape[axis + 1:]
    return jnp.swapaxes(w.reshape(split), axis + 1, axis + 2).reshape(shape)


def _attention_bias():
    i = np.arange(CHUNK)[:, None]
    j = np.arange(2 * CHUNK)[None, :]
    dist = (i + CHUNK - j).astype(np.float32)
    in_band = (dist >= 0) & (dist < CHUNK)
    slopes = np.array([2.0 ** (-8.0 * (h + 1) / N_Q_HEADS) for h in range(N_Q_HEADS)], dtype=np.float32)
    bias = -slopes[:, None, None] * dist[None]
    general = np.where(in_band[None], bias, -np.inf)
    first = np.where((in_band & (j >= CHUNK))[None], bias, -np.inf)
    return np.stack([general, first]).astype(np.float32)


def _mixer_b_kernel(sinks_ref, h_ref, kvc_ref, kvp_ref, g_ref, wq_ref, wo_ref, bias_ref, o_ref,
                    q_ref, att_ref):
    tm = h_ref.shape[0]
    blocks_per_seq = SEQ // CHUNK
    blocks_per_tile = tm // CHUNK
    kv_width = N_KV_HEADS * HEAD_DIM

    h = h_ref[...]
    inv_rms = lax.rsqrt(jnp.mean(h * h, axis=-1, keepdims=True) + EPS)
    q = _dot((h * g_ref[...]).astype(_BF16), wq_ref[...]) * (inv_rms * (HEAD_DIM ** -0.5))
    q_ref[...] = q.astype(_BF16)

    lane = lax.broadcasted_iota(jnp.int32, (2 * CHUNK, LANES), 1)
    low_half = lane < HEAD_DIM
    low_half_q = lax.broadcasted_iota(jnp.int32, (CHUNK, LANES), 1) < HEAD_DIM
    for b in range(blocks_per_tile):
        rows = slice(b * CHUNK, (b + 1) * CHUNK)
        block = pl.program_id(0) * blocks_per_tile + b
        first = jnp.where(block % blocks_per_seq == 0, 1, 0)
        if b == 0:
            kv_prev = kvp_ref[...]
        else:
            kv_prev = kvc_ref[(b - 1) * CHUNK:b * CHUNK, :]
        kv = jnp.concatenate([kv_prev, kvc_ref[rows, :]], axis=0)
        for pair in range(N_KV_HEADS // 2):
            k_pair = kv[:, pair * LANES:(pair + 1) * LANES]
            v_pair = kv[:, kv_width + pair * LANES:kv_width + (pair + 1) * LANES]
            groups = [pair * GQA_GROUP + g for g in range(GQA_GROUP)]
            q_stack = jnp.concatenate([q_ref[rows, j * LANES:(j + 1) * LANES] for j in groups], axis=0)
            weights, recips = [], []
            for half in range(2):
                kv_head = 2 * pair + half
                k_half = jnp.where(low_half if half == 0 else ~low_half, k_pair, 0)
                s = lax.dot_general(q_stack, k_half, (((1,), (1,)), ((), ())),
                                    preferred_element_type=_F32)
                parts, rparts = [], []
                for g in range(GQA_GROUP):
                    head = kv_head * GQA_GROUP + g
                    sg = s[g * CHUNK:(g + 1) * CHUNK, :] + bias_ref[first, head]
                    sink = sinks_ref[head]
                    mx = jnp.maximum(jnp.max(sg, axis=-1, keepdims=True), sink)
                    e = jnp.exp(sg - mx)
                    denom = jnp.sum(e, axis=-1, keepdims=True) + jnp.exp(sink - mx)
                    parts.append(e.astype(_BF16))
                    rparts.append(1.0 / denom)
                weights.append(jnp.concatenate(parts, axis=0))
                recips.append(rparts)
            p_cat = jnp.concatenate(weights, axis=1)
            v_cat = jnp.concatenate([jnp.where(low_half, v_pair, 0), jnp.where(low_half, 0, v_pair)], axis=0)
            out = _dot(p_cat, v_cat)
            for g in range(GQA_GROUP):
                j = groups[g]
                out_g = out[g * CHUNK:(g + 1) * CHUNK, :]
                att_ref[rows, j * LANES:(j + 1) * LANES] = jnp.where(
                    low_half_q, out_g * recips[0][g], out_g * recips[1][g]).astype(_BF16)
    o_ref[...] = h_ref[...] + _dot(att_ref[...], wo_ref[...])


def _mixer_b(h, kv, g, wq, wo, sinks, bias):
    m = h.shape[0]
    tm = TM_B
    blocks_per_tile = tm // CHUNK
    kvw = kv.shape[1]
    resident = [_nbytes(wq.shape, _BF16), _nbytes(wo.shape, _BF16), _nbytes(bias.shape, _F32)]
    streamed = [2 * _nbytes((tm, D_MODEL), _F32), _nbytes((tm + CHUNK, kvw), _BF16)]
    scratch = [2 * _nbytes((tm, D_MODEL), _BF16)]
    temporaries = [3 * _nbytes((tm, D_MODEL), _F32), 8 * _nbytes((GQA_GROUP * CHUNK, 2 * CHUNK), _F32)]

    return pl.pallas_call(
        _mixer_b_kernel,
        out_shape=jax.ShapeDtypeStruct((m, D_MODEL), _F32),
        grid=(m // tm,),
        in_specs=[
            pl.BlockSpec(memory_space=pltpu.SMEM),
            _row_spec(tm, D_MODEL),
            _row_spec(tm, kvw),
            pl.BlockSpec((CHUNK, kvw), lambda i: (jnp.maximum(i * blocks_per_tile - 1, 0), 0)),
            _const_spec(g.shape),
            _const_spec(wq.shape),
            _const_spec(wo.shape),
            _const_spec(bias.shape),
        ],
        out_specs=_row_spec(tm, D_MODEL),
        scratch_shapes=[pltpu.VMEM((tm, D_MODEL), _BF16), pltpu.VMEM((tm, D_MODEL), _BF16)],
        compiler_params=pltpu.CompilerParams(
            dimension_semantics=("arbitrary",),
            vmem_limit_bytes=_vmem_limit(resident, streamed, scratch, temporaries)),
        name="mixer_b",
    )(sinks, h, kv, kv, g, wq, wo, bias)


def kernel(x, p, norm_mix, norm_ffn, norm_ple, norm_kv, norm_final, a_w_in, a_norm_v, a_w_s, a_b_s, a_w_out, w_kv, b_w_q, b_sinks, b_w_o, f_w_up, f_conv_w, f_conv_b, f_w_down, ple_w_in, ple_w_gate, ple_b_gate):
    batch, seq, d = x.shape
    assert (seq, d) == (SEQ, D_MODEL) and seq % max(TM_A, TM_F, TM_B) == 0
    m = batch * seq
    row = lambda a: a.reshape(1, -1)

    h = x.reshape(m, d)
    bias_s = jnp.repeat(a_b_s[0].T, LANES, axis=1)
    h = _mixer_a(h, row(norm_mix[0]), a_w_in[0].astype(_BF16), row(a_norm_v[0]), a_w_s[0], bias_s,
                 a_w_out[0].astype(_BF16))

    rows = lambda a: a[:, None, :]
    stacked = (rows(norm_ffn), f_w_up.astype(_BF16), f_conv_w, rows(f_conv_b), f_w_down.astype(_BF16),
               rows(norm_ple), ple_w_in.astype(_BF16), ple_w_gate.astype(_BF16), rows(ple_b_gate))
    p = p.reshape(p.shape[0], m, PLE_DIM)

    h, kv = _ffn_ple(h, p, 0, stacked, row(norm_kv), w_kv.astype(_BF16), name="ffn_ple_0")

    h = _mixer_b(h, kv, row(norm_mix[1]), _pair_heads(b_w_q[0], 1).astype(_BF16),
                 _pair_heads(b_w_o[0], 0).astype(_BF16), b_sinks[0], jnp.asarray(_attention_bias()))

    out = _ffn_ple(h, p, 1, stacked, row(norm_final), name="ffn_ple_1")
    return out.reshape(batch, seq, d)
```

```python
import functools
import math

import jax
import jax.numpy as jnp
import numpy as np
from jax import lax
from jax.experimental import pallas as pl
from jax.experimental.pallas import tpu as pltpu

D_MODEL = 1024
SEQ = 8192
CHUNK = 128
A_GROUPS = 8
HEAD_DIM = 64
N_Q_HEADS = 16
N_KV_HEADS = 4
GQA_GROUP = N_Q_HEADS // N_KV_HEADS
D_FF = 2816
CONV_WIDTH = 3
PLE_DIM = 256
EPS = 1e-6

LANES = 128
SUBLANES = 8
MXU_WIDTH = 256
V7X_VMEM_BYTES = 64 * 1024 * 1024

TM_A = 1024
TM_F = 1024
TM_B = 1024
FF_CHUNK = MXU_WIDTH
N_FF_CHUNKS = D_FF // FF_CHUNK
GROUP_TILES = 8
GROUP = SUBLANES * GROUP_TILES
CARRY_ROWS = (CONV_WIDTH - 1) * SUBLANES

_BF16 = jnp.bfloat16
_F32 = jnp.float32


def _dot(a, b):
    return jnp.dot(a, b, preferred_element_type=_F32)


def _rms(x, g):
    return x * lax.rsqrt(jnp.mean(x * x, axis=-1, keepdims=True) + EPS) * g


def _silu(x):
    half = 0.5 * x
    return half * (jnp.tanh(half) + 1.0)


def _const_spec(shape):
    zeros = (0,) * len(shape)
    return pl.BlockSpec(shape, lambda i: zeros, pipeline_mode=pl.Buffered(1))


def _row_spec(tm, width):
    return pl.BlockSpec((tm, width), lambda i: (i, 0))


def _nbytes(shape, dtype):
    return math.prod(shape) * jnp.dtype(dtype).itemsize


def _vmem_limit(resident, streamed, scratch, temporaries):
    total = sum(resident) + 2 * sum(streamed) + sum(scratch) + sum(temporaries)
    return min(int(total), V7X_VMEM_BYTES)


def _mixer_a_kernel(x_ref, g_ref, w_in_ref, gv_ref, ws_ref, bs_ref, w_out_ref, o_ref,
                    z_ref, v_ref, gated_ref):
    tm = x_ref.shape[0]
    xn = _rms(x_ref[...], g_ref[...]).astype(_BF16)
    z = _dot(xn, w_in_ref[...])
    z_ref[...] = jax.nn.gelu(z, approximate=True)
    v_ref[...] = _rms(z_ref[:, D_MODEL:], gv_ref[...]).astype(_BF16)

    row = lax.broadcasted_iota(jnp.int32, (CHUNK, CHUNK), 0)
    col = lax.broadcasted_iota(jnp.int32, (CHUNK, CHUNK), 1)
    causal = row >= col
    for h in range(A_GROUPS):
        w_h = jnp.where(causal, ws_ref[h], 0.0).astype(_BF16)
        lanes = slice(h * LANES, (h + 1) * LANES)
        for c in range(tm // CHUNK):
            rows = slice(c * CHUNK, (c + 1) * CHUNK)
            s = _dot(w_h, v_ref[rows, lanes]) + bs_ref[:, lanes]
            gated_ref[rows, lanes] = (z_ref[rows, lanes] * s).astype(_BF16)
    o_ref[...] = x_ref[...] + _dot(gated_ref[...], w_out_ref[...])


def _mixer_a(x, g, w_in, gv, ws, bs, w_out):
    m = x.shape[0]
    tm = TM_A
    resident = [_nbytes(w_in.shape, _BF16), _nbytes(w_out.shape, _BF16), _nbytes(ws.shape, _F32),
                _nbytes(bs.shape, _F32)]
    streamed = [2 * _nbytes((tm, D_MODEL), _F32)]
    scratch = [_nbytes((tm, 2 * D_MODEL), _F32), 2 * _nbytes((tm, D_MODEL), _BF16)]
    temporaries = [2 * _nbytes((tm, 2 * D_MODEL), _F32)]
    return pl.pallas_call(
        _mixer_a_kernel,
        out_shape=jax.ShapeDtypeStruct((m, D_MODEL), _F32),
        grid=(m // tm,),
        in_specs=[
            _row_spec(tm, D_MODEL),
            _const_spec(g.shape),
            _const_spec(w_in.shape),
            _const_spec(gv.shape),
            _const_spec(ws.shape),
            _const_spec(bs.shape),
            _const_spec(w_out.shape),
        ],
        out_specs=_row_spec(tm, D_MODEL),
        scratch_shapes=[
            pltpu.VMEM((tm, 2 * D_MODEL), _F32),
            pltpu.VMEM((tm, D_MODEL), _BF16),
            pltpu.VMEM((tm, D_MODEL), _BF16),
        ],
        compiler_params=pltpu.CompilerParams(
            dimension_semantics=("arbitrary",),
            vmem_limit_bytes=_vmem_limit(resident, streamed, scratch, temporaries)),
        name="mixer_a",
    )(x, g, w_in, gv, ws, bs, w_out)


def _ffn_ple_kernel(final_norm, h_ref, p_ref, gf_ref, wup_ref, cw_ref, cb_ref, wdown_ref,
                    gp_ref, wpi_ref, wg_ref, bg_ref, ge_ref, *rest):
    if final_norm:
        o_ref, slab_ref, hn_ref, carry_ref, act_ref = rest
    else:
        we_ref, o_ref, kv_ref, slab_ref, hn_ref, carry_ref, act_ref = rest
    tm = h_ref.shape[0]
    first_of_seq = (pl.program_id(0) % (SEQ // tm)) == 0
    n_groups = tm // GROUP
    n_slabs = D_MODEL // LANES
    chunk_tiles = FF_CHUNK // LANES
    lanes = lambda l: slice(l * LANES, (l + 1) * LANES)

    hn = _rms(h_ref[...], gf_ref[...])
    for j in range(n_slabs):
        slab_ref[j] = hn[:, lanes(j)]
    for j in range(n_slabs):
        for k in range(n_groups):
            for q in range(0, GROUP_TILES, 2):
                pair = [slab_ref[j, pl.ds(GROUP * k + q + d, SUBLANES, stride=GROUP_TILES), :] for d in range(2)]
                rows = slice(GROUP * k + SUBLANES * q, GROUP * k + SUBLANES * (q + 2))
                hn_ref[rows, lanes(j)] = jnp.concatenate(pair, axis=0).astype(_BF16)

    first_row = lax.broadcasted_iota(jnp.int32, (SUBLANES, LANES), 0) == 0
    tails = {}

    def conv_block(c, half, j, k):
        col = slab_ref.at[2 * chunk_tiles * (c % 2) + chunk_tiles * half + j]
        blk = col[GROUP * k:GROUP * (k + 1), :]
        prev = tails[c, half][:, lanes(j)] if k == 0 else col[GROUP * k - CARRY_ROWS:GROUP * k, :]
        start = half * D_FF + c * FF_CHUNK + j * LANES
        wl = slice(start, start + LANES)
        edge2, edge1 = (
            jnp.where(first_row, pltpu.roll(prev[s:s + SUBLANES], 1, 0),
                      pltpu.roll(blk[GROUP - 2 * SUBLANES + s:GROUP - SUBLANES + s], 1, 0))
            for s in (0, SUBLANES))
        back1 = jnp.concatenate([edge1, blk[:GROUP - SUBLANES]], axis=0)
        back2 = jnp.concatenate([edge2, edge1, blk[:GROUP - 2 * SUBLANES]], axis=0)
        w0, w1, w2 = (cw_ref[t:t + 1, wl] for t in range(CONV_WIDTH))
        return cb_ref[:, wl] + back2 * w0 + back1 * w1 + blk * w2

    for c in range(N_FF_CHUNKS + 1):
        if c < N_FF_CHUNKS:
            for half in range(2):
                start = half * D_FF + c * FF_CHUNK
                up = _dot(hn_ref[...], wup_ref[:, start:start + FF_CHUNK])
                tails[c, half] = jnp.where(first_of_seq, 0.0, carry_ref[half, c])
                carry_ref[half, c] = up[tm - CARRY_ROWS:, :]
                for j in range(chunk_tiles):
                    slab_ref[2 * chunk_tiles * (c % 2) + chunk_tiles * half + j] = up[:, lanes(j)]
        if c >= 1:
            for j in range(chunk_tiles):
                for k in range(n_groups):
                    act_ref[GROUP * k:GROUP * (k + 1), lanes((c - 1) * chunk_tiles + j)] = (
                        _silu(conv_block(c - 1, 0, j, k)) * conv_block(c - 1, 1, j, k)).astype(_BF16)

    down = _dot(act_ref[...], wdown_ref[...])
    for j in range(n_slabs):
        slab_ref[j] = down[:, lanes(j)]
    assert GROUP_TILES == SUBLANES
    natural = jnp.concatenate(
        [jnp.concatenate([slab_ref[j, pl.ds(GROUP * k + m, SUBLANES, stride=GROUP_TILES), :] for j in range(n_slabs)],
                         axis=1)
         for k in range(n_groups) for m in range(GROUP_TILES)], axis=0)
    h = h_ref[...] + natural

    hn = _rms(h, gp_ref[...]).astype(_BF16)
    gate = jax.nn.sigmoid(_dot(hn, wg_ref[...]) + bg_ref[...])
    h = h + _dot(p_ref[...].astype(_BF16), wpi_ref[...]) * gate

    if final_norm:
        o_ref[...] = _rms(h, ge_ref[...])
    else:
        o_ref[...] = h
        kv_ref[...] = _dot(_rms(h, ge_ref[...]).astype(_BF16), we_ref[...]).astype(_BF16)


def _layer_spec(shape, layer):
    zeros = (0,) * (len(shape) - 1)
    return pl.BlockSpec((None,) + tuple(shape[1:]), lambda i: (layer,) + zeros, pipeline_mode=pl.Buffered(1))


def _ffn_ple(h, p, layer, stacked, ge, we=None, *, name):
    m = h.shape[0]
    tm = TM_F
    final_norm = we is None
    consts = (ge,) + (() if final_norm else (we,))
    out_shape = [jax.ShapeDtypeStruct((m, D_MODEL), _F32)]
    out_specs = [_row_spec(tm, D_MODEL)]
    streamed = [2 * _nbytes((tm, D_MODEL), _F32), _nbytes((tm, PLE_DIM), _F32)]
    if not final_norm:
        out_shape.append(jax.ShapeDtypeStruct((m, we.shape[1]), _BF16))
        out_specs.append(_row_spec(tm, we.shape[1]))
        streamed.append(_nbytes((tm, we.shape[1]), _BF16))
    scratch_shapes = [
        pltpu.VMEM((D_MODEL // LANES, tm, LANES), _F32),
        pltpu.VMEM((tm, D_MODEL), _BF16),
        pltpu.VMEM((2, N_FF_CHUNKS, CARRY_ROWS, FF_CHUNK), _F32),
        pltpu.VMEM((tm, D_FF), _BF16),
    ]
    resident = [_nbytes(w.shape[1:], w.dtype) for w in stacked] + [_nbytes(w.shape, w.dtype) for w in consts]
    scratch = [_nbytes(s.shape, s.dtype) for s in scratch_shapes]
    temporaries = [4 * _nbytes((tm, D_MODEL), _F32), 2 * _nbytes((tm, 2 * FF_CHUNK), _F32)]
    outs = pl.pallas_call(
        functools.partial(_ffn_ple_kernel, final_norm),
        out_shape=out_shape,
        grid=(m // tm,),
        in_specs=([_row_spec(tm, D_MODEL), pl.BlockSpec((None, tm, PLE_DIM), lambda i: (layer, i, 0))]
                  + [_layer_spec(w.shape, layer) for w in stacked] + [_const_spec(w.shape) for w in consts]),
        out_specs=out_specs,
        scratch_shapes=scratch_shapes,
        compiler_params=pltpu.CompilerParams(
            dimension_semantics=("arbitrary",),
            vmem_limit_bytes=_vmem_limit(resident, streamed, scratch, temporaries)),
        name=name,
    )(h, p, *stacked, *consts)
    return outs[0] if final_norm else outs


def _pair_heads(w, axis):
    shape = w.shape
    split = shape[:axis] + (N_KV_HEADS // 2, 2, GQA_GROUP, HEAD_DIM) + shape[axis + 1:]
    return jnp.swapaxes(w.reshape(split), axis + 1, axis + 2).reshape(shape)


def _attention_bias():
    i = np.arange(CHUNK)[:, None]
    j = np.arange(2 * CHUNK)[None, :]
    dist = (i + CHUNK - j).astype(np.float32)
    in_band = (dist >= 0) & (dist < CHUNK)
    slopes = np.array([2.0 ** (-8.0 * (h + 1) / N_Q_HEADS) for h in range(N_Q_HEADS)], dtype=np.float32)
    bias = -slopes[:, None, None] * dist[None]
    general = np.where(in_band[None], bias, -np.inf)
    first = np.where((in_band & (j >= CHUNK))[None], bias, -np.inf)
    return np.stack([general, first]).astype(np.float32)


def _mixer_b_kernel(sinks_ref, h_ref, kvc_ref, kvp_ref, g_ref, wq_ref, wo_ref, bias_ref, o_ref,
                    q_ref, att_ref):
    tm = h_ref.shape[0]
    blocks_per_seq = SEQ // CHUNK
    blocks_per_tile = tm // CHUNK
    kv_width = N_KV_HEADS * HEAD_DIM

    q = _dot(_rms(h_ref[...], g_ref[...]).astype(_BF16), wq_ref[...]) * (HEAD_DIM ** -0.5)
    q_ref[...] = q.astype(_BF16)

    lane = lax.broadcasted_iota(jnp.int32, (2 * CHUNK, LANES), 1)
    low_half = lane < HEAD_DIM
    low_half_q = lax.broadcasted_iota(jnp.int32, (CHUNK, LANES), 1) < HEAD_DIM
    for b in range(blocks_per_tile):
        rows = slice(b * CHUNK, (b + 1) * CHUNK)
        block = pl.program_id(0) * blocks_per_tile + b
        first = jnp.where(block % blocks_per_seq == 0, 1, 0)
        if b == 0:
            kv_prev = kvp_ref[...]
        else:
            kv_prev = kvc_ref[(b - 1) * CHUNK:b * CHUNK, :]
        kv = jnp.concatenate([kv_prev, kvc_ref[rows, :]], axis=0)
        for pair in range(N_KV_HEADS // 2):
            k_pair = kv[:, pair * LANES:(pair + 1) * LANES]
            v_pair = kv[:, kv_width + pair * LANES:kv_width + (pair + 1) * LANES]
            groups = [pair * GQA_GROUP + g for g in range(GQA_GROUP)]
            q_stack = jnp.concatenate([q_ref[rows, j * LANES:(j + 1) * LANES] for j in groups], axis=0)
            weights, recips = [], []
            for half in range(2):
                kv_head = 2 * pair + half
                k_half = jnp.where(low_half if half == 0 else ~low_half, k_pair, 0)
                s = lax.dot_general(q_stack, k_half, (((1,), (1,)), ((), ())),
                                    preferred_element_type=_F32)
                parts, rparts = [], []
                for g in range(GQA_GROUP):
                    head = kv_head * GQA_GROUP + g
                    sg = s[g * CHUNK:(g + 1) * CHUNK, :] + bias_ref[first, head]
                    sink = sinks_ref[head]
                    mx = jnp.maximum(jnp.max(sg, axis=-1, keepdims=True), sink)
                    e = jnp.exp(sg - mx)
                    denom = jnp.sum(e, axis=-1, keepdims=True) + jnp.exp(sink - mx)
                    parts.append(e.astype(_BF16))
                    rparts.append(1.0 / denom)
                weights.append(jnp.concatenate(parts, axis=0))
                recips.append(rparts)
            p_cat = jnp.concatenate(weights, axis=1)
            v_cat = jnp.concatenate([jnp.where(low_half, v_pair, 0), jnp.where(low_half, 0, v_pair)], axis=0)
            out = _dot(p_cat, v_cat)
            for g in range(GQA_GROUP):
                j = groups[g]
                out_g = out[g * CHUNK:(g + 1) * CHUNK, :]
                att_ref[rows, j * LANES:(j + 1) * LANES] = jnp.where(
                    low_half_q, out_g * recips[0][g], out_g * recips[1][g]).astype(_BF16)
    o_ref[...] = h_ref[...] + _dot(att_ref[...], wo_ref[...])


def _mixer_b(h, kv, g, wq, wo, sinks, bias):
    m = h.shape[0]
    tm = TM_B
    blocks_per_tile = tm // CHUNK
    kvw = kv.shape[1]
    resident = [_nbytes(wq.shape, _BF16), _nbytes(wo.shape, _BF16), _nbytes(bias.shape, _F32)]
    streamed = [2 * _nbytes((tm, D_MODEL), _F32), _nbytes((tm + CHUNK, kvw), _BF16)]
    scratch = [2 * _nbytes((tm, D_MODEL), _BF16)]
    temporaries = [3 * _nbytes((tm, D_MODEL), _F32), 8 * _nbytes((GQA_GROUP * CHUNK, 2 * CHUNK), _F32)]

    return pl.pallas_call(
        _mixer_b_kernel,
        out_shape=jax.ShapeDtypeStruct((m, D_MODEL), _F32),
        grid=(m // tm,),
        in_specs=[
            pl.BlockSpec(memory_space=pltpu.SMEM),
            _row_spec(tm, D_MODEL),
            _row_spec(tm, kvw),
            pl.BlockSpec((CHUNK, kvw), lambda i: (jnp.maximum(i * blocks_per_tile - 1, 0), 0)),
            _const_spec(g.shape),
            _const_spec(wq.shape),
            _const_spec(wo.shape),
            _const_spec(bias.shape),
        ],
        out_specs=_row_spec(tm, D_MODEL),
        scratch_shapes=[pltpu.VMEM((tm, D_MODEL), _BF16), pltpu.VMEM((tm, D_MODEL), _BF16)],
        compiler_params=pltpu.CompilerParams(
            dimension_semantics=("arbitrary",),
            vmem_limit_bytes=_vmem_limit(resident, streamed, scratch, temporaries)),
        name="mixer_b",
    )(sinks, h, kv, kv, g, wq, wo, bias)


def kernel(x, p, norm_mix, norm_ffn, norm_ple, norm_kv, norm_final, a_w_in, a_norm_v, a_w_s, a_b_s, a_w_out, w_kv, b_w_q, b_sinks, b_w_o, f_w_up, f_conv_w, f_conv_b, f_w_down, ple_w_in, ple_w_gate, ple_b_gate):
    batch, seq, d = x.shape
    assert (seq, d) == (SEQ, D_MODEL) and seq % max(TM_A, TM_F, TM_B) == 0
    m = batch * seq
    row = lambda a: a.reshape(1, -1)

    h = x.reshape(m, d)
    bias_s = jnp.repeat(a_b_s[0].T, LANES, axis=1)
    h = _mixer_a(h, row(norm_mix[0]), a_w_in[0].astype(_BF16), row(a_norm_v[0]), a_w_s[0], bias_s,
                 a_w_out[0].astype(_BF16))

    rows = lambda a: a[:, None, :]
    stacked = (rows(norm_ffn), f_w_up.astype(_BF16), f_conv_w, rows(f_conv_b), f_w_down.astype(_BF16),
               rows(norm_ple), ple_w_in.astype(_BF16), ple_w_gate.astype(_BF16), rows(ple_b_gate))
    p = p.reshape(p.shape[0], m, PLE_DIM)

    h, kv = _ffn_ple(h, p, 0, stacked, row(norm_kv), w_kv.astype(_BF16), name="ffn_ple_0")

    h = _mixer_b(h, kv, row(norm_mix[1]), _pair_heads(b_w_q[0], 1).astype(_BF16),
                 _pair_heads(b_w_o[0], 0).astype(_BF16), b_sinks[0], jnp.asarray(_attention_bias()))

    out = _ffn_ple(h, p, 1, stacked, row(norm_final), name="ffn_ple_1")
    return out.reshape(batch, seq, d)
```

```python
import functools
import math

import jax
import jax.numpy as jnp
import numpy as np
from jax import lax
from jax.experimental import pallas as pl
from jax.experimental.pallas import tpu as pltpu

D_MODEL = 1024
SEQ = 8192
CHUNK = 128
A_GROUPS = 8
HEAD_DIM = 64
N_Q_HEADS = 16
N_KV_HEADS = 4
GQA_GROUP = N_Q_HEADS // N_KV_HEADS
D_FF = 2816
CONV_WIDTH = 3
PLE_DIM = 256
EPS = 1e-6

LANES = 128
SUBLANES = 8
MXU_WIDTH = 256
V7X_VMEM_BYTES = 64 * 1024 * 1024

TM_A = 1024
TM_F = 1024
TM_B = 1024
FF_CHUNK = MXU_WIDTH
N_FF_CHUNKS = D_FF // FF_CHUNK
GROUP_TILES = 8
GROUP = SUBLANES * GROUP_TILES
CARRY_ROWS = (CONV_WIDTH - 1) * SUBLANES

_BF16 = jnp.bfloat16
_F32 = jnp.float32


def _dot(a, b):
    return jnp.dot(a, b, preferred_element_type=_F32)


def _rms(x, g):
    return x * lax.rsqrt(jnp.mean(x * x, axis=-1, keepdims=True) + EPS) * g


def _silu(x):
    half = 0.5 * x
    return half * (jnp.tanh(half) + 1.0)


def _const_spec(shape):
    zeros = (0,) * len(shape)
    return pl.BlockSpec(shape, lambda i: zeros, pipeline_mode=pl.Buffered(1))


def _row_spec(tm, width):
    return pl.BlockSpec((tm, width), lambda i: (i, 0))


def _nbytes(shape, dtype):
    return math.prod(shape) * jnp.dtype(dtype).itemsize


def _vmem_limit(resident, streamed, scratch, temporaries):
    total = sum(resident) + 2 * sum(streamed) + sum(scratch) + sum(temporaries)
    return min(int(total), V7X_VMEM_BYTES)


def _mixer_a_kernel(x_ref, g_ref, w_in_ref, gv_ref, ws_ref, bs_ref, w_out_ref, o_ref,
                    z_ref, v_ref, gated_ref):
    tm = x_ref.shape[0]
    xn = _rms(x_ref[...], g_ref[...]).astype(_BF16)
    z = _dot(xn, w_in_ref[...])
    z_ref[...] = jax.nn.gelu(z, approximate=True)
    v_ref[...] = _rms(z_ref[:, D_MODEL:], gv_ref[...]).astype(_BF16)

    row = lax.broadcasted_iota(jnp.int32, (CHUNK, CHUNK), 0)
    col = lax.broadcasted_iota(jnp.int32, (CHUNK, CHUNK), 1)
    causal = row >= col
    for h in range(A_GROUPS):
        w_h = jnp.where(causal, ws_ref[h], 0.0).astype(_BF16)
        lanes = slice(h * LANES, (h + 1) * LANES)
        for c in range(tm // CHUNK):
            rows = slice(c * CHUNK, (c + 1) * CHUNK)
            s = _dot(w_h, v_ref[rows, lanes]) + bs_ref[:, lanes]
            gated_ref[rows, lanes] = (z_ref[rows, lanes] * s).astype(_BF16)
    o_ref[...] = x_ref[...] + _dot(gated_ref[...], w_out_ref[...])


def _mixer_a(x, g, w_in, gv, ws, bs, w_out):
    m = x.shape[0]
    tm = TM_A
    resident = [_nbytes(w_in.shape, _BF16), _nbytes(w_out.shape, _BF16), _nbytes(ws.shape, _F32),
                _nbytes(bs.shape, _F32)]
    streamed = [2 * _nbytes((tm, D_MODEL), _F32)]
    scratch = [_nbytes((tm, 2 * D_MODEL), _F32), 2 * _nbytes((tm, D_MODEL), _BF16)]
    temporaries = [2 * _nbytes((tm, 2 * D_MODEL), _F32)]
    return pl.pallas_call(
        _mixer_a_kernel,
        out_shape=jax.ShapeDtypeStruct((m, D_MODEL), _F32),
        grid=(m // tm,),
        in_specs=[
            _row_spec(tm, D_MODEL),
            _const_spec(g.shape),
            _const_spec(w_in.shape),
            _const_spec(gv.shape),
            _const_spec(ws.shape),
            _const_spec(bs.shape),
            _const_spec(w_out.shape),
        ],
        out_specs=_row_spec(tm, D_MODEL),
        scratch_shapes=[
            pltpu.VMEM((tm, 2 * D_MODEL), _F32),
            pltpu.VMEM((tm, D_MODEL), _BF16),
            pltpu.VMEM((tm, D_MODEL), _BF16),
        ],
        compiler_params=pltpu.CompilerParams(
            dimension_semantics=("arbitrary",),
            vmem_limit_bytes=_vmem_limit(resident, streamed, scratch, temporaries)),
        name="mixer_a",
    )(x, g, w_in, gv, ws, bs, w_out)


def _ffn_ple_kernel(final_norm, h_ref, p_ref, gf_ref, wup_ref, cw_ref, cb_ref, wdown_ref,
                    gp_ref, wpi_ref, wg_ref, bg_ref, ge_ref, *rest):
    if final_norm:
        o_ref, slab_ref, hn_ref, carry_ref, act_ref = rest
    else:
        we_ref, o_ref, kv_ref, slab_ref, hn_ref, carry_ref, act_ref = rest
    tm = h_ref.shape[0]
    first_of_seq = (pl.program_id(0) % (SEQ // tm)) == 0
    n_groups = tm // GROUP
    n_slabs = D_MODEL // LANES
    chunk_tiles = FF_CHUNK // LANES
    lanes = lambda l: slice(l * LANES, (l + 1) * LANES)

    hn = _rms(h_ref[...], gf_ref[...])
    for j in range(n_slabs):
        slab_ref[j] = hn[:, lanes(j)]
    for j in range(n_slabs):
        for k in range(n_groups):
            for q in range(0, GROUP_TILES, 2):
                pair = [slab_ref[j, pl.ds(GROUP * k + q + d, SUBLANES, stride=GROUP_TILES), :] for d in range(2)]
                rows = slice(GROUP * k + SUBLANES * q, GROUP * k + SUBLANES * (q + 2))
                hn_ref[rows, lanes(j)] = jnp.concatenate(pair, axis=0).astype(_BF16)

    first_row = lax.broadcasted_iota(jnp.int32, (SUBLANES, LANES), 0) == 0
    tails = {}

    def conv_block(c, half, j, k):
        col = slab_ref.at[2 * chunk_tiles * (c % 2) + chunk_tiles * half + j]
        blk = col[GROUP * k:GROUP * (k + 1), :]
        prev = tails[c, half][:, lanes(j)] if k == 0 else col[GROUP * k - CARRY_ROWS:GROUP * k, :]
        start = half * D_FF + c * FF_CHUNK + j * LANES
        wl = slice(start, start + LANES)
        edge2, edge1 = (
            jnp.where(first_row, pltpu.roll(prev[s:s + SUBLANES], 1, 0),
                      pltpu.roll(blk[GROUP - 2 * SUBLANES + s:GROUP - SUBLANES + s], 1, 0))
            for s in (0, SUBLANES))
        back1 = jnp.concatenate([edge1, blk[:GROUP - SUBLANES]], axis=0)
        back2 = jnp.concatenate([edge2, edge1, blk[:GROUP - 2 * SUBLANES]], axis=0)
        w0, w1, w2 = (cw_ref[t:t + 1, wl].astype(_BF16) for t in range(CONV_WIDTH))
        return (cb_ref[:, wl].astype(_BF16) + back2.astype(_BF16) * w0 + back1.astype(_BF16) * w1
                + blk.astype(_BF16) * w2)

    for c in range(N_FF_CHUNKS + 1):
        if c < N_FF_CHUNKS:
            for half in range(2):
                start = half * D_FF + c * FF_CHUNK
                up = _dot(hn_ref[...], wup_ref[:, start:start + FF_CHUNK])
                tails[c, half] = jnp.where(first_of_seq, 0.0, carry_ref[half, c])
                carry_ref[half, c] = up[tm - CARRY_ROWS:, :]
                for j in range(chunk_tiles):
                    slab_ref[2 * chunk_tiles * (c % 2) + chunk_tiles * half + j] = up[:, lanes(j)]
        if c >= 1:
            for j in range(chunk_tiles):
                for k in range(n_groups):
                    act_ref[GROUP * k:GROUP * (k + 1), lanes((c - 1) * chunk_tiles + j)] = (
                        _silu(conv_block(c - 1, 0, j, k)) * conv_block(c - 1, 1, j, k)).astype(_BF16)

    down = _dot(act_ref[...], wdown_ref[...])
    for j in range(n_slabs):
        for k in range(n_groups):
            for q in range(GROUP_TILES):
                rows = slice(GROUP * k + SUBLANES * q, GROUP * k + SUBLANES * (q + 1))
                slab_ref[j, pl.ds(GROUP * k + q, SUBLANES, stride=GROUP_TILES), :] = down[rows, lanes(j)]
    h = h_ref[...] + jnp.concatenate([slab_ref[j] for j in range(n_slabs)], axis=1)

    hn = _rms(h, gp_ref[...]).astype(_BF16)
    gate = jax.nn.sigmoid(_dot(hn, wg_ref[...]) + bg_ref[...])
    h = h + _dot(p_ref[...].astype(_BF16), wpi_ref[...]) * gate

    if final_norm:
        o_ref[...] = _rms(h, ge_ref[...])
    else:
        o_ref[...] = h
        kv_ref[...] = _dot(_rms(h, ge_ref[...]).astype(_BF16), we_ref[...]).astype(_BF16)


def _layer_spec(shape, layer):
    zeros = (0,) * (len(shape) - 1)
    return pl.BlockSpec((None,) + tuple(shape[1:]), lambda i: (layer,) + zeros, pipeline_mode=pl.Buffered(1))


def _ffn_ple(h, p, layer, stacked, ge, we=None, *, name):
    m = h.shape[0]
    tm = TM_F
    final_norm = we is None
    consts = (ge,) + (() if final_norm else (we,))
    out_shape = [jax.ShapeDtypeStruct((m, D_MODEL), _F32)]
    out_specs = [_row_spec(tm, D_MODEL)]
    streamed = [2 * _nbytes((tm, D_MODEL), _F32), _nbytes((tm, PLE_DIM), _F32)]
    if not final_norm:
        out_shape.append(jax.ShapeDtypeStruct((m, we.shape[1]), _BF16))
        out_specs.append(_row_spec(tm, we.shape[1]))
        streamed.append(_nbytes((tm, we.shape[1]), _BF16))
    scratch_shapes = [
        pltpu.VMEM((D_MODEL // LANES, tm, LANES), _F32),
        pltpu.VMEM((tm, D_MODEL), _BF16),
        pltpu.VMEM((2, N_FF_CHUNKS, CARRY_ROWS, FF_CHUNK), _F32),
        pltpu.VMEM((tm, D_FF), _BF16),
    ]
    resident = [_nbytes(w.shape[1:], w.dtype) for w in stacked] + [_nbytes(w.shape, w.dtype) for w in consts]
    scratch = [_nbytes(s.shape, s.dtype) for s in scratch_shapes]
    temporaries = [4 * _nbytes((tm, D_MODEL), _F32), 2 * _nbytes((tm, 2 * FF_CHUNK), _F32)]
    outs = pl.pallas_call(
        functools.partial(_ffn_ple_kernel, final_norm),
        out_shape=out_shape,
        grid=(m // tm,),
        in_specs=([_row_spec(tm, D_MODEL), pl.BlockSpec((None, tm, PLE_DIM), lambda i: (layer, i, 0))]
                  + [_layer_spec(w.shape, layer) for w in stacked] + [_const_spec(w.shape) for w in consts]),
        out_specs=out_specs,
        scratch_shapes=scratch_shapes,
        compiler_params=pltpu.CompilerParams(
            dimension_semantics=("arbitrary",),
            vmem_limit_bytes=_vmem_limit(resident, streamed, scratch, temporaries)),
        name=name,
    )(h, p, *stacked, *consts)
    return outs[0] if final_norm else outs


def _pair_heads(w, axis):
    shape = w.shape
    split = shape[:axis] + (N_KV_HEADS // 2, 2, GQA_GROUP, HEAD_DIM) + shape[axis + 1:]
    return jnp.swapaxes(w.reshape(split), axis + 1, axis + 2).reshape(shape)


def _attention_bias():
    i = np.arange(CHUNK)[:, None]
    j = np.arange(2 * CHUNK)[None, :]
    dist = (i + CHUNK - j).astype(np.float32)
    in_band = (dist >= 0) & (dist < CHUNK)
    slopes = np.array([2.0 ** (-8.0 * (h + 1) / N_Q_HEADS) for h in range(N_Q_HEADS)], dtype=np.float32)
    bias = -slopes[:, None, None] * dist[None]
    general = np.where(in_band[None], bias, -np.inf)
    first = np.where((in_band & (j >= CHUNK))[None], bias, -np.inf)
    return np.stack([general, first]).astype(np.float32)


def _mixer_b_kernel(sinks_ref, h_ref, kvc_ref, kvp_ref, g_ref, wq_ref, wo_ref, bias_ref, o_ref,
                    q_ref, att_ref):
    tm = h_ref.shape[0]
    blocks_per_seq = SEQ // CHUNK
    blocks_per_tile = tm // CHUNK
    kv_width = N_KV_HEADS * HEAD_DIM

    q = _dot(_rms(h_ref[...], g_ref[...]).astype(_BF16), wq_ref[...]) * (HEAD_DIM ** -0.5)
    q_ref[...] = q.astype(_BF16)

    lane = lax.broadcasted_iota(jnp.int32, (2 * CHUNK, LANES), 1)
    low_half = lane < HEAD_DIM
    low_half_q = lax.broadcasted_iota(jnp.int32, (CHUNK, LANES), 1) < HEAD_DIM
    for b in range(blocks_per_tile):
        rows = slice(b * CHUNK, (b + 1) * CHUNK)
        block = pl.program_id(0) * blocks_per_tile + b
        first = jnp.where(block % blocks_per_seq == 0, 1, 0)
        if b == 0:
            kv_prev = kvp_ref[...]
        else:
            kv_prev = kvc_ref[(b - 1) * CHUNK:b * CHUNK, :]
        kv = jnp.concatenate([kv_prev, kvc_ref[rows, :]], axis=0)
        for pair in range(N_KV_HEADS // 2):
            k_pair = kv[:, pair * LANES:(pair + 1) * LANES]
            v_pair = kv[:, kv_width + pair * LANES:kv_width + (pair + 1) * LANES]
            groups = [pair * GQA_GROUP + g for g in range(GQA_GROUP)]
            q_stack = jnp.concatenate([q_ref[rows, j * LANES:(j + 1) * LANES] for j in groups], axis=0)
            weights, recips = [], []
            for half in range(2):
                kv_head = 2 * pair + half
                k_half = jnp.where(low_half if half == 0 else ~low_half, k_pair, 0)
                s = lax.dot_general(q_stack, k_half, (((1,), (1,)), ((), ())),
                                    preferred_element_type=_F32)
                parts, rparts = [], []
                for g in range(GQA_GROUP):
                    head = kv_head * GQA_GROUP + g
                    sg = s[g * CHUNK:(g + 1) * CHUNK, :] + bias_ref[first, head]
                    sink = sinks_ref[head]
                    mx = jnp.maximum(jnp.max(sg, axis=-1, keepdims=True), sink)
                    e = jnp.exp(sg - mx)
                    denom = jnp.sum(e, axis=-1, keepdims=True) + jnp.exp(sink - mx)
                    parts.append(e.astype(_BF16))
                    rparts.append(1.0 / denom)
                weights.append(jnp.concatenate(parts, axis=0))
                recips.append(rparts)
            p_cat = jnp.concatenate(weights, axis=1)
            v_cat = jnp.concatenate([jnp.where(low_half, v_pair, 0), jnp.where(low_half, 0, v_pair)], axis=0)
            out = _dot(p_cat, v_cat)
            for g in range(GQA_GROUP):
                j = groups[g]
                out_g = out[g * CHUNK:(g + 1) * CHUNK, :]
                att_ref[rows, j * LANES:(j + 1) * LANES] = jnp.where(
                    low_half_q, out_g * recips[0][g], out_g * recips[1][g]).astype(_BF16)
    o_ref[...] = h_ref[...] + _dot(att_ref[...], wo_ref[...])


def _mixer_b(h, kv, g, wq, wo, sinks, bias):
    m = h.shape[0]
    tm = TM_B
    blocks_per_tile = tm // CHUNK
    kvw = kv.shape[1]
    resident = [_nbytes(wq.shape, _BF16), _nbytes(wo.shape, _BF16), _nbytes(bias.shape, _F32)]
    streamed = [2 * _nbytes((tm, D_MODEL), _F32), _nbytes((tm + CHUNK, kvw), _BF16)]
    scratch = [2 * _nbytes((tm, D_MODEL), _BF16)]
    temporaries = [3 * _nbytes((tm, D_MODEL), _F32), 8 * _nbytes((GQA_GROUP * CHUNK, 2 * CHUNK), _F32)]

    return pl.pallas_call(
        _mixer_b_kernel,
        out_shape=jax.ShapeDtypeStruct((m, D_MODEL), _F32),
        grid=(m // tm,),
        in_specs=[
            pl.BlockSpec(memory_space=pltpu.SMEM),
            _row_spec(tm, D_MODEL),
            _row_spec(tm, kvw),
            pl.BlockSpec((CHUNK, kvw), lambda i: (jnp.maximum(i * blocks_per_tile - 1, 0), 0)),
            _const_spec(g.shape),
            _const_spec(wq.shape),
            _const_spec(wo.shape),
            _const_spec(bias.shape),
        ],
        out_specs=_row_spec(tm, D_MODEL),
        scratch_shapes=[pltpu.VMEM((tm, D_MODEL), _BF16), pltpu.VMEM((tm, D_MODEL), _BF16)],
        compiler_params=pltpu.CompilerParams(
            dimension_semantics=("arbitrary",),
            vmem_limit_bytes=_vmem_limit(resident, streamed, scratch, temporaries)),
        name="mixer_b",
    )(sinks, h, kv, kv, g, wq, wo, bias)


def kernel(x, p, norm_mix, norm_ffn, norm_ple, norm_kv, norm_final, a_w_in, a_norm_v, a_w_s, a_b_s, a_w_out, w_kv, b_w_q, b_sinks, b_w_o, f_w_up, f_conv_w, f_conv_b, f_w_down, ple_w_in, ple_w_gate, ple_b_gate):
    batch, seq, d = x.shape
    assert (seq, d) == (SEQ, D_MODEL) and seq % max(TM_A, TM_F, TM_B) == 0
    m = batch * seq
    row = lambda a: a.reshape(1, -1)

    h = x.reshape(m, d)
    bias_s = jnp.repeat(a_b_s[0].T, LANES, axis=1)
    h = _mixer_a(h, row(norm_mix[0]), a_w_in[0].astype(_BF16), row(a_norm_v[0]), a_w_s[0], bias_s,
                 a_w_out[0].astype(_BF16))

    rows = lambda a: a[:, None, :]
    stacked = (rows(norm_ffn), f_w_up.astype(_BF16), f_conv_w, rows(f_conv_b), f_w_down.astype(_BF16),
               rows(norm_ple), ple_w_in.astype(_BF16), ple_w_gate.astype(_BF16), rows(ple_b_gate))
    p = p.reshape(p.shape[0], m, PLE_DIM)

    h, kv = _ffn_ple(h, p, 0, stacked, row(norm_kv), w_kv.astype(_BF16), name="ffn_ple_0")

    h = _mixer_b(h, kv, row(norm_mix[1]), _pair_heads(b_w_q[0], 1).astype(_BF16),
                 _pair_heads(b_w_o[0], 0).astype(_BF16), b_sinks[0], jnp.asarray(_attention_bias()))

    out = _ffn_ple(h, p, 1, stacked, row(norm_final), name="ffn_ple_1")
    return out.reshape(batch, seq, d)
```
